```python
import jax, jax.numpy as jnp
from jax import lax
import numpy as np

D_MODEL = 1024
BATCH = 32
SEQ = 2048
DEPTH = 4

D_CONV_A = 512
CONV_A_WIDTH = 3
D_CONV_B = 512
CONV_B_WIDTH = 31
N_HEADS = 16
HEAD_DIM = 64
D_ATTN = N_HEADS * HEAD_DIM
BLOCK_Q = 128
N_GROUPS = 4
EXPERTS_PER_GROUP = 8
N_EXPERTS = N_GROUPS * EXPERTS_PER_GROUP
TOP_K = 2
D_EXPERT = 512
ROW_CHUNK = 256
LN_EPS = 1e-5
DEEPNORM_ALPHA = (2 * DEPTH) ** 0.25
DEEPNORM_BETA = (8 * DEPTH) ** -0.25

N_EVEN = (DEPTH + 1) // 2
N_ODD = DEPTH // 2
D_IN_EVEN = 3 * D_CONV_A + 2 * D_CONV_B
D_MIX_EVEN = D_CONV_A + D_CONV_B

kernel_name = "hybrid_conv_stickbreak_hmoe_deepnorm"


def layer_norm(x, g, b):
    xf = x.astype(jnp.float32)
    mu = jnp.mean(xf, axis=-1, keepdims=True)
    var = jnp.mean(jnp.square(xf - mu), axis=-1, keepdims=True)
    y = (xf - mu) * lax.rsqrt(var + LN_EPS)
    return (y * g + b).astype(x.dtype)


def causal_depthwise_conv(x, w):
    k = w.shape[0]
    return lax.conv_general_dilated(
        x, w[:, None, :].astype(x.dtype), window_strides=(1,), padding=[(k - 1, 0)],
        dimension_numbers=("NWC", "WIO", "NWC"), feature_group_count=x.shape[-1])


def conv_mixers(x, w_in, conv_a, conv_b_w, conv_b_bias, norm_b_g, norm_b_b, w_out):
    u = jnp.einsum("bsd,de->bse", x, w_in)
    a_b, a_c, a_v, b_val, b_gate = jnp.split(
        u, [D_CONV_A, 2 * D_CONV_A, 3 * D_CONV_A, 3 * D_CONV_A + D_CONV_B], axis=-1)
    y_a = a_b * causal_depthwise_conv(a_c * a_v, conv_a)
    g = b_val * jax.nn.sigmoid(b_gate)
    g = causal_depthwise_conv(g, conv_b_w) + conv_b_bias
    y_b = jax.nn.silu(layer_norm(g, norm_b_g, norm_b_b))
    return jnp.einsum("bse,ed->bsd", jnp.concatenate([y_a, y_b], axis=-1), w_out)


def stick_breaking_attention(x, w_qkv, w_o):
    b, s, _ = x.shape
    qkv = jnp.einsum("bsd,de->bse", x, w_qkv).reshape(b, s, 3, N_HEADS, HEAD_DIM)
    q = qkv[:, :, 0].transpose(0, 2, 1, 3)
    k = qkv[:, :, 1].transpose(0, 2, 1, 3)
    v = qkv[:, :, 2].transpose(0, 2, 1, 3)
    scale = HEAD_DIM ** -0.5
    outs = []
    for blk in range(s // BLOCK_Q):
        q0 = blk * BLOCK_Q
        kv_len = q0 + BLOCK_Q
        qb = q[:, :, q0:kv_len]
        kb = k[:, :, :kv_len]
        vb = v[:, :, :kv_len]
        z = jnp.einsum("bhqd,bhkd->bhqk", qb, kb,
                       preferred_element_type=jnp.float32) * scale
        t_idx = q0 + jnp.arange(BLOCK_Q)[:, None]
        s_idx = jnp.arange(kv_len)[None, :]
        causal = s_idx < t_idx
        log_keep = jnp.where(causal, jax.nn.log_sigmoid(-z), 0.0)
        later = lax.cumsum(log_keep, axis=3, reverse=True) - log_keep
        a = jnp.where(causal, jnp.exp(jax.nn.log_sigmoid(z) + later), 0.0)
        outs.append(jnp.einsum("bhqk,bhkd->bhqd", a.astype(vb.dtype), vb))
    o = jnp.concatenate(outs, axis=2).transpose(0, 2, 1, 3).reshape(b, s, D_ATTN)
    return jnp.einsum("bse,ed->bsd", o, w_o)


def hierarchical_moe(x, rg_w, rg_b, re_w, re_b, w_up, w_down):
    b, s, d = x.shape
    h = x.reshape(-1, d)
    t = h.shape[0]
    g_logits = jnp.matmul(h, rg_w).astype(jnp.float32) + rg_b
    grp = jnp.argmax(g_logits, axis=-1)
    g_gate = jnp.take_along_axis(jax.nn.softmax(g_logits, axis=-1), grp[:, None], axis=-1)
    e_logits = (jnp.matmul(h, re_w).astype(jnp.float32) + re_b).reshape(t, N_GROUPS, EXPERTS_PER_GROUP)
    e_logits = jnp.take_along_axis(e_logits, grp[:, None, None], axis=1)[:, 0]
    top_val, top_idx = lax.top_k(e_logits, TOP_K)
    weights = (g_gate * jax.nn.softmax(top_val, axis=-1)).reshape(-1)
    ids = (grp[:, None] * EXPERTS_PER_GROUP + top_idx).reshape(-1).astype(jnp.int32)
    n_assign = t * TOP_K
    order = jnp.argsort(ids)
    sorted_ids = ids[order]
    tok = order // TOP_K
    sizes = jnp.bincount(ids, length=N_EXPERTS)
    padded = (sizes + ROW_CHUNK - 1) // ROW_CHUNK * ROW_CHUNK
    pad_end = jnp.cumsum(padded)
    pad_start = pad_end - padded
    start = jnp.cumsum(sizes) - sizes
    dest = pad_start[sorted_ids] + jnp.arange(n_assign) - start[sorted_ids]
    n_chunks = -(-n_assign // ROW_CHUNK) + N_EXPERTS
    rows = jnp.zeros((n_chunks * ROW_CHUNK, d), h.dtype).at[dest].set(h[tok])
    chunk_expert = jnp.minimum(
        jnp.searchsorted(pad_end, jnp.arange(n_chunks) * ROW_CHUNK, side="right"), N_EXPERTS - 1)

    def expert_chunk(args):
        xc, e = args
        gate, up = jnp.split(jnp.matmul(xc, w_up[e]), 2, axis=-1)
        return jnp.matmul(jax.nn.silu(gate) * up, w_down[e])

    y_rows = lax.map(expert_chunk, (rows.reshape(n_chunks, ROW_CHUNK, d), chunk_expert))
    y = y_rows.reshape(-1, d)[dest] * weights[order][:, None].astype(h.dtype)
    out = jnp.zeros((t, d), h.dtype).at[tok].add(y)
    return out.reshape(b, s, d)


def setup_inputs(seed: int = 0) -> dict:
    key = jax.random.key(seed)
    ks = jax.random.split(key, 24)
    nrm = jax.random.normal
    f32 = jnp.float32
    x = nrm(ks[0], (BATCH, SEQ, D_MODEL), f32)
    even_w_in = nrm(ks[1], (N_EVEN, D_MODEL, D_IN_EVEN), f32) * D_MODEL ** -0.5
    even_conv_a = nrm(ks[2], (N_EVEN, CONV_A_WIDTH, D_CONV_A), f32) * CONV_A_WIDTH ** -0.5
    even_conv_b_w = nrm(ks[3], (N_EVEN, CONV_B_WIDTH, D_CONV_B), f32) * CONV_B_WIDTH ** -0.5
    even_conv_b_bias = nrm(ks[4], (N_EVEN, D_CONV_B), f32) * 0.02
    even_norm_b_g = 1.0 + 0.1 * nrm(ks[5], (N_EVEN, D_CONV_B), f32)
    even_norm_b_b = 0.02 * nrm(ks[6], (N_EVEN, D_CONV_B), f32)
    even_w_out = nrm(ks[7], (N_EVEN, D_MIX_EVEN, D_MODEL), f32) * (D_MIX_EVEN ** -0.5 * DEEPNORM_BETA)
    v_scale = jnp.concatenate([jnp.ones((2 * D_ATTN,), f32), jnp.full((D_ATTN,), DEEPNORM_BETA, f32)])
    odd_w_qkv = nrm(ks[8], (N_ODD, D_MODEL, 3 * D_ATTN), f32) * D_MODEL ** -0.5 * v_scale
    odd_w_o = nrm(ks[9], (N_ODD, D_ATTN, D_MODEL), f32) * (D_ATTN ** -0.5 * DEEPNORM_BETA)
    ln_mix_g = 1.0 + 0.1 * nrm(ks[10], (DEPTH, D_MODEL), f32)
    ln_mix_b = 0.02 * nrm(ks[11], (DEPTH, D_MODEL), f32)
    ln_ffn_g = 1.0 + 0.1 * nrm(ks[12], (DEPTH, D_MODEL), f32)
    ln_ffn_b = 0.02 * nrm(ks[13], (DEPTH, D_MODEL), f32)
    router_group_w = nrm(ks[14], (DEPTH, D_MODEL, N_GROUPS), f32) * D_MODEL ** -0.5
    router_group_b = 0.01 * nrm(ks[15], (DEPTH, N_GROUPS), f32)
    router_expert_w = nrm(ks[16], (DEPTH, D_MODEL, N_EXPERTS), f32) * D_MODEL ** -0.5
    router_expert_b = 0.01 * nrm(ks[17], (DEPTH, N_EXPERTS), f32)
    expert_w_up = nrm(ks[18], (DEPTH, N_EXPERTS, D_MODEL, 2 * D_EXPERT), f32) * D_MODEL ** -0.5
    expert_w_down = nrm(ks[19], (DEPTH, N_EXPERTS, D_EXPERT, D_MODEL), f32) * (D_EXPERT ** -0.5 * DEEPNORM_BETA)
    return {"x": x, "even_w_in": even_w_in, "even_conv_a": even_conv_a,
            "even_conv_b_w": even_conv_b_w, "even_conv_b_bias": even_conv_b_bias,
            "even_norm_b_g": even_norm_b_g, "even_norm_b_b": even_norm_b_b,
            "even_w_out": even_w_out, "odd_w_qkv": odd_w_qkv, "odd_w_o": odd_w_o,
            "ln_mix_g": ln_mix_g, "ln_mix_b": ln_mix_b, "ln_ffn_g": ln_ffn_g, "ln_ffn_b": ln_ffn_b,
            "router_group_w": router_group_w, "router_group_b": router_group_b,
            "router_expert_w": router_expert_w, "router_expert_b": router_expert_b,
            "expert_w_up": expert_w_up, "expert_w_down": expert_w_down}


def reference(x, even_w_in, even_conv_a, even_conv_b_w, even_conv_b_bias, even_norm_b_g,
              even_norm_b_b, even_w_out, odd_w_qkv, odd_w_o, ln_mix_g, ln_mix_b, ln_ffn_g,
              ln_ffn_b, router_group_w, router_group_b, router_expert_w, router_expert_b,
              expert_w_up, expert_w_down):
    for layer in range(DEPTH):
        i = layer // 2
        if layer % 2 == 0:
            mix = conv_mixers(x, even_w_in[i], even_conv_a[i], even_conv_b_w[i],
                              even_conv_b_bias[i], even_norm_b_g[i], even_norm_b_b[i],
                              even_w_out[i])
        else:
            mix = stick_breaking_attention(x, odd_w_qkv[i], odd_w_o[i])
        x = layer_norm(DEEPNORM_ALPHA * x + mix, ln_mix_g[layer], ln_mix_b[layer])
        ffn = hierarchical_moe(x, router_group_w[layer], router_group_b[layer],
                               router_expert_w[layer], router_expert_b[layer],
                               expert_w_up[layer], expert_w_down[layer])
        x = layer_norm(DEEPNORM_ALPHA * x + ffn, ln_ffn_g[layer], ln_ffn_b[layer])
    return x
```

```python
import functools

import jax
import jax.numpy as jnp
from jax import lax
from jax.experimental import pallas as pl
from jax.experimental.pallas import tpu as pltpu

F32 = jnp.float32
BF16 = jnp.bfloat16

D_MODEL = 1024
DEPTH = 4
D_CONV_A = 512
CONV_A_WIDTH = 3
D_CONV_B = 512
CONV_B_WIDTH = 31
N_HEADS = 16
HEAD_DIM = 64
D_ATTN = N_HEADS * HEAD_DIM
N_GROUPS = 4
EXPERTS_PER_GROUP = 8
N_EXPERTS = N_GROUPS * EXPERTS_PER_GROUP
TOP_K = 2
D_EXPERT = 512
LN_EPS = 1e-5
DEEPNORM_ALPHA = (2 * DEPTH) ** 0.25
D_IN_EVEN = 3 * D_CONV_A + 2 * D_CONV_B

LANES = 128
ROW_TILE = 512
CONV_HALO = 32
CONV_ROWS = 32
ATTN_TQ = 512
ATTN_TK = 256
EXPERT_CHUNK = 512
COMBINE_TILE = 256
ROUTER_COLS = LANES
VMEM_LIMIT = 56 * 1024 * 1024


def _layer_norm(v, g, b):
    mu = jnp.mean(v, axis=-1, keepdims=True)
    c = v - mu
    var = jnp.mean(c * c, axis=-1, keepdims=True)
    return c * lax.rsqrt(var + LN_EPS) * g + b


def _router_logits(h, rw_ref):
    hi = h.astype(BF16)
    lo = (h - hi.astype(F32)).astype(BF16)
    r1 = jnp.dot(hi, rw_ref[...], preferred_element_type=F32)
    r2 = jnp.dot(lo, rw_ref[:, :ROUTER_COLS], preferred_element_type=F32)
    return r1[:, :ROUTER_COLS] + r1[:, ROUTER_COLS:] + r2


def _finish(x, mix, lng_ref, lnb_ref, rw_ref, h_ref, logit_ref):
    h = _layer_norm(DEEPNORM_ALPHA * x + mix, lng_ref[...], lnb_ref[...])
    h_ref[0] = h
    logit_ref[0] = _router_logits(h, rw_ref)


def _even_kernel(x_ref, win_ref, ca_ref, cbw_ref, cbb_ref, nbg_ref, nbb_ref, wout_ref,
                 lng_ref, lnb_ref, rw_ref, h_ref, logit_ref, u_buf, cv_buf, g_buf, y_buf):
    s = pl.program_id(1)
    ts = x_ref.shape[1]
    x = x_ref[0]
    u_buf[...] = jnp.dot(x.astype(BF16), win_ref[...], preferred_element_type=F32)

    @pl.when(s == 0)
    def _():
        cv_buf[0:CONV_HALO] = jnp.zeros((CONV_HALO, D_CONV_A), F32)
        g_buf[0:CONV_HALO] = jnp.zeros((CONV_HALO, D_CONV_B), F32)

    @pl.when(s > 0)
    def _():
        cv_buf[0:CONV_HALO] = cv_buf[ts:ts + CONV_HALO]
        g_buf[0:CONV_HALO] = g_buf[ts:ts + CONV_HALO]

    c0 = D_CONV_A
    cv_buf[CONV_HALO:CONV_HALO + ts] = u_buf[:, c0:2 * c0] * u_buf[:, 2 * c0:3 * c0]
    b_val = u_buf[:, 3 * c0:3 * c0 + D_CONV_B]
    b_gate = u_buf[:, 3 * c0 + D_CONV_B:]
    g_buf[CONV_HALO:CONV_HALO + ts] = b_val * jax.nn.sigmoid(b_gate)

    for r0 in range(0, ts, CONV_ROWS):
        acc_a = None
        for k in range(CONV_A_WIDTH):
            off = CONV_HALO + r0 - (CONV_A_WIDTH - 1) + k
            term = ca_ref[k:k + 1, :] * cv_buf[off:off + CONV_ROWS, :]
            acc_a = term if acc_a is None else acc_a + term
        y_a = u_buf[r0:r0 + CONV_ROWS, 0:c0] * acc_a
        acc_b = None
        for k in range(CONV_B_WIDTH):
            off = CONV_HALO + r0 - (CONV_B_WIDTH - 1) + k
            term = cbw_ref[k:k + 1, :] * g_buf[off:off + CONV_ROWS, :]
            acc_b = term if acc_b is None else acc_b + term
        gb = _layer_norm(acc_b + cbb_ref[...], nbg_ref[...], nbb_ref[...])
        y_b = gb * jax.nn.sigmoid(gb)
        y_buf[r0:r0 + CONV_ROWS, 0:c0] = y_a.astype(BF16)
        y_buf[r0:r0 + CONV_ROWS, c0:] = y_b.astype(BF16)

    mix = jnp.dot(y_buf[...], wout_ref[...], preferred_element_type=F32)
    _finish(x, mix, lng_ref, lnb_ref, rw_ref, h_ref, logit_ref)


def _const_spec(shape):
    return pl.BlockSpec(shape, lambda *_: (0,) * len(shape))


def _even_mixer(x, w_in, conv_a, conv_b_w, conv_b_bias, norm_b_g, norm_b_b, w_out, ln_g, ln_b, rw):
    b, s, d = x.shape
    ts = min(ROW_TILE, s)
    row = lambda v: v.reshape(1, -1)
    return pl.pallas_call(
        _even_kernel,
        grid=(b, s // ts),
        in_specs=[
            pl.BlockSpec((1, ts, d), lambda i, j: (i, j, 0)),
            _const_spec((d, D_IN_EVEN)),
            _const_spec((CONV_A_WIDTH, D_CONV_A)),
            _const_spec((CONV_B_WIDTH, D_CONV_B)),
            _const_spec((1, D_CONV_B)),
            _const_spec((1, D_CONV_B)),
            _const_spec((1, D_CONV_B)),
            _const_spec((D_CONV_A + D_CONV_B, d)),
            _const_spec((1, d)),
            _const_spec((1, d)),
            _const_spec((d, 2 * ROUTER_COLS)),
        ],
        out_specs=[
            pl.BlockSpec((1, ts, d), lambda i, j: (i, j, 0)),
            pl.BlockSpec((1, ts, ROUTER_COLS), lambda i, j: (i, j, 0)),
        ],
        out_shape=[
            jax.ShapeDtypeStruct((b, s, d), F32),
            jax.ShapeDtypeStruct((b, s, ROUTER_COLS), F32),
        ],
        scratch_shapes=[
            pltpu.VMEM((ts, D_IN_EVEN), F32),
            pltpu.VMEM((CONV_HALO + ts, D_CONV_A), F32),
            pltpu.VMEM((CONV_HALO + ts, D_CONV_B), F32),
            pltpu.VMEM((ts, D_CONV_A + D_CONV_B), BF16),
        ],
        compiler_params=pltpu.CompilerParams(
            dimension_semantics=("arbitrary", "arbitrary"), vmem_limit_bytes=VMEM_LIMIT),
    )(x, w_in.astype(BF16), conv_a, conv_b_w, row(conv_b_bias), row(norm_b_g), row(norm_b_b),
      w_out.astype(BF16), row(ln_g), row(ln_b), rw)


def _qkv_kernel(x_ref, w_ref, o_ref):
    u = jnp.dot(x_ref[0].astype(BF16), w_ref[...], preferred_element_type=F32)
    o_ref[0, :, :D_ATTN] = (u[:, :D_ATTN] * (HEAD_DIM ** -0.5)).astype(BF16)
    o_ref[0, :, D_ATTN:] = u[:, D_ATTN:].astype(BF16)


def _qkv_proj(x, w_qkv):
    b, s, d = x.shape
    ts = min(ROW_TILE, s)
    return pl.pallas_call(
        _qkv_kernel,
        grid=(b, s // ts),
        in_specs=[pl.BlockSpec((1, ts, d), lambda i, j: (i, j, 0)), _const_spec((d, 3 * D_ATTN))],
        out_specs=pl.BlockSpec((1, ts, 3 * D_ATTN), lambda i, j: (i, j, 0)),
        out_shape=jax.ShapeDtypeStruct((b, s, 3 * D_ATTN), BF16),
        compiler_params=pltpu.CompilerParams(
            dimension_semantics=("arbitrary", "arbitrary"), vmem_limit_bytes=VMEM_LIMIT),
    )(x, w_qkv.astype(BF16))


def _attn_kernel(q_ref, k_ref, v_ref, o_ref, acc_ref, carry_ref, *, tq, tk):
    s_len = q_ref.shape[1]
    lane = lax.broadcasted_iota(jnp.int32, (1, LANES), 1)
    tri = (lax.broadcasted_iota(jnp.int32, (tk, tk), 0)
           >= lax.broadcasted_iota(jnp.int32, (tk, tk), 1)).astype(BF16)
    row_id = lax.broadcasted_iota(jnp.int32, (tq, tk), 0)
    col_id = lax.broadcasted_iota(jnp.int32, (tq, tk), 1)

    for head in range(LANES // HEAD_DIM):
        head_lanes = (lane // HEAD_DIM) == head
        for q0 in range(0, s_len, tq):
            qm = jnp.where(head_lanes, q_ref[0, q0:q0 + tq, :], jnp.zeros((), BF16))
            acc_ref[...] = jnp.zeros_like(acc_ref)
            carry_ref[...] = jnp.zeros_like(carry_ref)

            def block(k0, diag_offset):
                kb = k_ref[0, pl.ds(k0, tk), :]
                vb = v_ref[0, pl.ds(k0, tk), :]
                z = lax.dot_general(qm, kb, (((1,), (1,)), ((), ())), preferred_element_type=F32)
                sp = jnp.maximum(z, 0.0) + jnp.log(1.0 + jnp.exp(-jnp.abs(z)))
                if diag_offset is not None:
                    causal = col_id + diag_offset < row_id
                    sp = jnp.where(causal, sp, 0.0)
                hi = sp.astype(BF16)
                lo = (sp - hi.astype(F32)).astype(BF16)
                c = (jnp.dot(hi, tri, preferred_element_type=F32)
                     + jnp.dot(lo, tri, preferred_element_type=F32))
                carry = carry_ref[...]
                a = jnp.exp(z - (c + jnp.concatenate([carry] * (tk // LANES), axis=1)))
                if diag_offset is not None:
                    a = jnp.where(causal, a, 0.0)
                acc_ref[...] += jnp.dot(a.astype(BF16), vb, preferred_element_type=F32)
                carry_ref[...] = carry + jnp.broadcast_to(c[:, 0:1], carry.shape)

            for k0 in range(q0 + tq - tk, q0 - tk, -tk):
                block(k0, k0 - q0)

            def body(i, _):
                block(pl.multiple_of(q0 - tk * (i + 1), tk), None)
                return 0

            if q0 > 0:
                lax.fori_loop(0, q0 // tk, body, 0)
            lo_l, hi_l = head * HEAD_DIM, (head + 1) * HEAD_DIM
            o_ref[0, q0:q0 + tq, lo_l:hi_l] = acc_ref[:, lo_l:hi_l].astype(BF16)


def _attention(qkv):
    b, s, _ = qkv.shape
    tq = min(ATTN_TQ, s)
    tk = min(ATTN_TK, tq)
    n_pairs = D_ATTN // LANES
    return pl.pallas_call(
        functools.partial(_attn_kernel, tq=tq, tk=tk),
        grid=(b, n_pairs),
        in_specs=[
            pl.BlockSpec((1, s, LANES), lambda i, p: (i, 0, p)),
            pl.BlockSpec((1, s, LANES), lambda i, p: (i, 0, n_pairs + p)),
            pl.BlockSpec((1, s, LANES), lambda i, p: (i, 0, 2 * n_pairs + p)),
        ],
        out_specs=pl.BlockSpec((1, s, LANES), lambda i, p: (i, 0, p)),
        out_shape=jax.ShapeDtypeStruct((b, s, D_ATTN), BF16),
        scratch_shapes=[pltpu.VMEM((tq, LANES), F32), pltpu.VMEM((tq, LANES), F32)],
        compiler_params=pltpu.CompilerParams(
            dimension_semantics=("arbitrary", "arbitrary"), vmem_limit_bytes=VMEM_LIMIT),
    )(qkv, qkv, qkv)


def _oproj_kernel(o_ref, x_ref, w_ref, lng_ref, lnb_ref, rw_ref, h_ref, logit_ref):
    mix = jnp.dot(o_ref[0], w_ref[...], preferred_element_type=F32)
    _finish(x_ref[0], mix, lng_ref, lnb_ref, rw_ref, h_ref, logit_ref)


def _attn_out(o, x, w_o, ln_g, ln_b, rw):
    b, s, d = x.shape
    ts = min(ROW_TILE, s)
    row = lambda v: v.reshape(1, -1)
    return pl.pallas_call(
        _oproj_kernel,
        grid=(b, s // ts),
        in_specs=[
            pl.BlockSpec((1, ts, D_ATTN), lambda i, j: (i, j, 0)),
            pl.BlockSpec((1, ts, d), lambda i, j: (i, j, 0)),
            _const_spec((D_ATTN, d)),
            _const_spec((1, d)),
            _const_spec((1, d)),
            _const_spec((d, 2 * ROUTER_COLS)),
        ],
        out_specs=[
            pl.BlockSpec((1, ts, d), lambda i, j: (i, j, 0)),
            pl.BlockSpec((1, ts, ROUTER_COLS), lambda i, j: (i, j, 0)),
        ],
        out_shape=[
            jax.ShapeDtypeStruct((b, s, d), F32),
            jax.ShapeDtypeStruct((b, s, ROUTER_COLS), F32),
        ],
        compiler_params=pltpu.CompilerParams(
            dimension_semantics=("arbitrary", "arbitrary"), vmem_limit_bytes=VMEM_LIMIT),
    )(o, x, w_o.astype(BF16), row(ln_g), row(ln_b), rw)


def _routing(logits, rg_b, re_b, chunk):
    t = logits.shape[0]
    g_logits = logits[:, :N_GROUPS] + rg_b
    grp = jnp.argmax(g_logits, axis=-1)
    g_gate = jnp.take_along_axis(jax.nn.softmax(g_logits, axis=-1), grp[:, None], axis=-1)
    e_logits = (logits[:, N_GROUPS:N_GROUPS + N_EXPERTS] + re_b).reshape(t, N_GROUPS, EXPERTS_PER_GROUP)
    e_logits = jnp.take_along_axis(e_logits, grp[:, None, None], axis=1)[:, 0]
    top_val, top_idx = lax.top_k(e_logits, TOP_K)
    weights = g_gate * jax.nn.softmax(top_val, axis=-1)
    ids = (grp[:, None] * EXPERTS_PER_GROUP + top_idx).reshape(-1).astype(jnp.int32)

    n_assign = t * TOP_K
    onehot = (ids[:, None] == jnp.arange(N_EXPERTS, dtype=jnp.int32)[None, :]).astype(jnp.int32)
    csum = jnp.cumsum(onehot, axis=0)
    sizes = csum[-1]
    rank = jnp.take_along_axis(csum, ids[:, None], axis=1)[:, 0] - 1
    padded = (sizes + chunk - 1) // chunk * chunk
    pad_end = jnp.cumsum(padded)
    pad_start = pad_end - padded
    dest = (pad_start[ids] + rank).astype(jnp.int32)
    n_chunks = -(-n_assign // chunk) + N_EXPERTS
    tok = jnp.arange(n_assign, dtype=jnp.int32) // TOP_K
    row_tok = jnp.zeros((n_chunks * chunk,), jnp.int32).at[dest].set(tok)
    chunk_expert = jnp.minimum(
        jnp.searchsorted(pad_end, jnp.arange(n_chunks, dtype=jnp.int32) * chunk, side="right"),
        N_EXPERTS - 1).astype(jnp.int32)
    n_valid = (pad_end[-1] // chunk).astype(jnp.int32).reshape(1)
    return weights.astype(F32), dest, row_tok, chunk_expert, n_valid, n_chunks


def _expert_kernel(ce_ref, nv_ref, tok_ref, h_hbm, wup_ref, wdn_ref, y_ref,
                   xbuf, wup_bf, wdn_bf, sem):
    i = pl.program_id(0)
    chunk = xbuf.shape[0]

    @pl.when(i < nv_ref[0])
    def _():
        def issue(j, _):
            pltpu.make_async_copy(
                h_hbm.at[pl.ds(tok_ref[0, 0, j], 1), :], xbuf.at[pl.ds(j, 1), :], sem).start()
            return 0

        lax.fori_loop(0, chunk, issue, 0)

        prev = ce_ref[jnp.maximum(i - 1, 0)]

        @pl.when((i == 0) | (ce_ref[i] != prev))
        def _():
            wup_bf[...] = wup_ref[0, 0].astype(BF16)
            wdn_bf[...] = wdn_ref[0, 0].astype(BF16)

        pltpu.make_async_copy(h_hbm.at[pl.ds(0, chunk), :], xbuf, sem).wait()
        gu = jnp.dot(xbuf[...].astype(BF16), wup_bf[...], preferred_element_type=F32)
        gate, up = gu[:, :D_EXPERT], gu[:, D_EXPERT:]
        act = gate * jax.nn.sigmoid(gate) * up
        y_ref[...] = jnp.dot(act.astype(BF16), wdn_bf[...], preferred_element_type=F32)

    @pl.when(i >= nv_ref[0])
    def _():
        y_ref[...] = jnp.zeros_like(y_ref)


def _experts(h2d, row_tok, chunk_expert, n_valid, n_chunks, chunk, layer, w_up, w_down):
    t, d = h2d.shape
    grid_spec = pltpu.PrefetchScalarGridSpec(
        num_scalar_prefetch=2,
        grid=(n_chunks,),
        in_specs=[
            pl.BlockSpec((1, 1, chunk), lambda i, ce, nv: (i, 0, 0), memory_space=pltpu.SMEM),
            pl.BlockSpec(memory_space=pl.ANY),
            pl.BlockSpec((1, 1, d, 2 * D_EXPERT), lambda i, ce, nv: (layer, ce[i], 0, 0)),
            pl.BlockSpec((1, 1, D_EXPERT, d), lambda i, ce, nv: (layer, ce[i], 0, 0)),
        ],
        out_specs=pl.BlockSpec((chunk, d), lambda i, ce, nv: (i, 0)),
        scratch_shapes=[
            pltpu.VMEM((chunk, d), F32),
            pltpu.VMEM((d, 2 * D_EXPERT), BF16),
            pltpu.VMEM((D_EXPERT, d), BF16),
            pltpu.SemaphoreType.DMA(()),
        ],
    )
    return pl.pallas_call(
        _expert_kernel,
        grid_spec=grid_spec,
        out_shape=jax.ShapeDtypeStruct((n_chunks * chunk, d), F32),
        compiler_params=pltpu.CompilerParams(
            dimension_semantics=("arbitrary",), vmem_limit_bytes=VMEM_LIMIT),
    )(chunk_expert, n_valid, row_tok.reshape(n_chunks, 1, chunk), h2d, w_up, w_down)


def _combine_kernel(pos_ref, h_ref, w_ref, y_hbm, lng_ref, lnb_ref, o_ref, ybuf, sem):
    tm = h_ref.shape[0]

    def issue(j, _):
        for k in range(TOP_K):
            pltpu.make_async_copy(
                y_hbm.at[pl.ds(pos_ref[0, 0, TOP_K * j + k], 1), :],
                ybuf.at[pl.ds(k * tm + j, 1), :], sem).start()
        return 0

    lax.fori_loop(0, tm, issue, 0)
    pltpu.make_async_copy(y_hbm.at[pl.ds(0, TOP_K * tm), :], ybuf, sem).wait()
    w = w_ref[...]
    ffn = w[:, 0:1] * ybuf[0:tm, :]
    for k in range(1, TOP_K):
        ffn = ffn + w[:, k:k + 1] * ybuf[k * tm:(k + 1) * tm, :]
    o_ref[...] = _layer_norm(DEEPNORM_ALPHA * h_ref[...] + ffn, lng_ref[...], lnb_ref[...])


def _combine(h2d, weights, dest, y_rows, ln_g, ln_b):
    t, d = h2d.shape
    tm = min(COMBINE_TILE, t)
    row = lambda v: v.reshape(1, -1)
    return pl.pallas_call(
        _combine_kernel,
        grid=(t // tm,),
        in_specs=[
            pl.BlockSpec((1, 1, TOP_K * tm), lambda i: (i, 0, 0), memory_space=pltpu.SMEM),
            pl.BlockSpec((tm, d), lambda i: (i, 0)),
            pl.BlockSpec((tm, TOP_K), lambda i: (i, 0)),
            pl.BlockSpec(memory_space=pl.ANY),
            _const_spec((1, d)),
            _const_spec((1, d)),
        ],
        out_specs=pl.BlockSpec((tm, d), lambda i: (i, 0)),
        out_shape=jax.ShapeDtypeStruct((t, d), F32),
        scratch_shapes=[pltpu.VMEM((TOP_K * tm, d), F32), pltpu.SemaphoreType.DMA(())],
        compiler_params=pltpu.CompilerParams(
            dimension_semantics=("arbitrary",), vmem_limit_bytes=VMEM_LIMIT),
    )(dest.reshape(t // tm, 1, TOP_K * tm), h2d, weights, y_rows, row(ln_g), row(ln_b))


def _moe_block(h, logits, rg_b, re_b, layer, w_up, w_down, ln_g, ln_b):
    b, s, d = h.shape
    t = b * s
    chunk = min(EXPERT_CHUNK, t)
    weights, dest, row_tok, chunk_expert, n_valid, n_chunks = _routing(
        logits.reshape(t, ROUTER_COLS), rg_b, re_b, chunk)
    h2d = h.reshape(t, d)
    y_rows = _experts(h2d, row_tok, chunk_expert, n_valid, n_chunks, chunk, layer, w_up, w_down)
    return _combine(h2d, weights, dest, y_rows, ln_g, ln_b).reshape(b, s, d)


def _router_weights(rg_w, re_w):
    w = jnp.concatenate([rg_w, re_w], axis=1)
    w = jnp.pad(w, ((0, 0), (0, ROUTER_COLS - w.shape[1])))
    hi = w.astype(BF16)
    lo = (w - hi.astype(F32)).astype(BF16)
    return jnp.concatenate([hi, lo], axis=1)


def kernel(x, even_w_in, even_conv_a, even_conv_b_w, even_conv_b_bias, even_norm_b_g,
           even_norm_b_b, even_w_out, odd_w_qkv, odd_w_o, ln_mix_g, ln_mix_b, ln_ffn_g,
           ln_ffn_b, router_group_w, router_group_b, router_expert_w, router_expert_b,
           expert_w_up, expert_w_down):
    for layer in range(DEPTH):
        i = layer // 2
        rw = _router_weights(router_group_w[layer], router_expert_w[layer])
        if layer % 2 == 0:
            h, logits = _even_mixer(x, even_w_in[i], even_conv_a[i], even_conv_b_w[i],
                                    even_conv_b_bias[i], even_norm_b_g[i], even_norm_b_b[i],
                                    even_w_out[i], ln_mix_g[layer], ln_mix_b[layer], rw)
        else:
            o = _attention(_qkv_proj(x, odd_w_qkv[i]))
            h, logits = _attn_out(o, x, odd_w_o[i], ln_mix_g[layer], ln_mix_b[layer], rw)
        x = _moe_block(h, logits, router_group_b[layer], router_expert_b[layer], layer,
                       expert_w_up, expert_w_down, ln_ffn_g[layer], ln_ffn_b[layer])
    return x
```

```python
import functools

import jax
import jax.numpy as jnp
from jax import lax
from jax.experimental import pallas as pl
from jax.experimental.pallas import tpu as pltpu
from jax.experimental.pallas import tpu_sc as plsc

F32 = jnp.float32
BF16 = jnp.bfloat16

D_MODEL = 1024
DEPTH = 4
D_CONV_A = 512
CONV_A_WIDTH = 3
D_CONV_B = 512
CONV_B_WIDTH = 31
N_HEADS = 16
HEAD_DIM = 64
D_ATTN = N_HEADS * HEAD_DIM
N_GROUPS = 4
EXPERTS_PER_GROUP = 8
N_EXPERTS = N_GROUPS * EXPERTS_PER_GROUP
TOP_K = 2
D_EXPERT = 512
LN_EPS = 1e-5
DEEPNORM_ALPHA = (2 * DEPTH) ** 0.25
D_IN_EVEN = 3 * D_CONV_A + 2 * D_CONV_B

LANES = 128
ROW_TILE = 512
CONV_HALO = 32
CONV_ROWS = 32
ATTN_BLOCK = 256
ATTN_HEADS = 4
EXPERT_CHUNK = 512
SC_GATHER_ROWS = 32
ROUTER_COLS = LANES
VMEM_LIMIT = 56 * 1024 * 1024


def _layer_norm(v, g, b):
    mu = jnp.mean(v, axis=-1, keepdims=True)
    c = v - mu
    var = jnp.mean(c * c, axis=-1, keepdims=True)
    return c * lax.rsqrt(var + LN_EPS) * g + b


def _router_logits(h, rw_ref):
    hi = h.astype(BF16)
    lo = (h - hi.astype(F32)).astype(BF16)
    r1 = jnp.dot(hi, rw_ref[...], preferred_element_type=F32)
    r2 = jnp.dot(lo, rw_ref[:, :ROUTER_COLS], preferred_element_type=F32)
    return r1[:, :ROUTER_COLS] + r1[:, ROUTER_COLS:] + r2


def _finish(x, mix, lng_ref, lnb_ref, rw_ref, h_ref, logit_ref):
    h = _layer_norm(DEEPNORM_ALPHA * x + mix, lng_ref[...], lnb_ref[...])
    h_ref[0] = h
    logit_ref[0] = _router_logits(h, rw_ref)


def _even_kernel(x_ref, win_ref, ca_ref, cbw_ref, cbb_ref, nbg_ref, nbb_ref, wout_ref,
                 lng_ref, lnb_ref, rw_ref, h_ref, logit_ref, u_buf, cv_buf, g_buf, y_buf):
    s = pl.program_id(1)
    ts = x_ref.shape[1]
    x = x_ref[0]
    u_buf[...] = jnp.dot(x.astype(BF16), win_ref[...], preferred_element_type=F32)

    @pl.when(s == 0)
    def _():
        cv_buf[0:CONV_HALO] = jnp.zeros((CONV_HALO, D_CONV_A), F32)
        g_buf[0:CONV_HALO] = jnp.zeros((CONV_HALO, D_CONV_B), F32)

    @pl.when(s > 0)
    def _():
        cv_buf[0:CONV_HALO] = cv_buf[ts:ts + CONV_HALO]
        g_buf[0:CONV_HALO] = g_buf[ts:ts + CONV_HALO]

    c0 = D_CONV_A
    cv_buf[CONV_HALO:CONV_HALO + ts] = u_buf[:, c0:2 * c0] * u_buf[:, 2 * c0:3 * c0]
    b_val = u_buf[:, 3 * c0:3 * c0 + D_CONV_B]
    b_gate = u_buf[:, 3 * c0 + D_CONV_B:]
    g_buf[CONV_HALO:CONV_HALO + ts] = b_val * jax.nn.sigmoid(b_gate)

    for r0 in range(0, ts, CONV_ROWS):
        acc_a = None
        for k in range(CONV_A_WIDTH):
            off = CONV_HALO + r0 - (CONV_A_WIDTH - 1) + k
            term = ca_ref[k:k + 1, :] * cv_buf[off:off + CONV_ROWS, :]
            acc_a = term if acc_a is None else acc_a + term
        y_a = u_buf[r0:r0 + CONV_ROWS, 0:c0] * acc_a
        acc_b = None
        for k in range(CONV_B_WIDTH):
            off = CONV_HALO + r0 - (CONV_B_WIDTH - 1) + k
            term = cbw_ref[k:k + 1, :] * g_buf[off:off + CONV_ROWS, :]
            acc_b = term if acc_b is None else acc_b + term
        gb = _layer_norm(acc_b + cbb_ref[...], nbg_ref[...], nbb_ref[...])
        y_b = gb * jax.nn.sigmoid(gb)
        y_buf[r0:r0 + CONV_ROWS, 0:c0] = y_a.astype(BF16)
        y_buf[r0:r0 + CONV_ROWS, c0:] = y_b.astype(BF16)

    mix = jnp.dot(y_buf[...], wout_ref[...], preferred_element_type=F32)
    _finish(x, mix, lng_ref, lnb_ref, rw_ref, h_ref, logit_ref)


def _const_spec(shape):
    return pl.BlockSpec(shape, lambda *_: (0,) * len(shape))


def _even_mixer(x, w_in, conv_a, conv_b_w, conv_b_bias, norm_b_g, norm_b_b, w_out, ln_g, ln_b, rw):
    b, s, d = x.shape
    ts = min(ROW_TILE, s)
    row = lambda v: v.reshape(1, -1)
    return pl.pallas_call(
        _even_kernel,
        grid=(b, s // ts),
        in_specs=[
            pl.BlockSpec((1, ts, d), lambda i, j: (i, j, 0)),
            _const_spec((d, D_IN_EVEN)),
            _const_spec((CONV_A_WIDTH, D_CONV_A)),
            _const_spec((CONV_B_WIDTH, D_CONV_B)),
            _const_spec((1, D_CONV_B)),
            _const_spec((1, D_CONV_B)),
            _const_spec((1, D_CONV_B)),
            _const_spec((D_CONV_A + D_CONV_B, d)),
            _const_spec((1, d)),
            _const_spec((1, d)),
            _const_spec((d, 2 * ROUTER_COLS)),
        ],
        out_specs=[
            pl.BlockSpec((1, ts, d), lambda i, j: (i, j, 0)),
            pl.BlockSpec((1, ts, ROUTER_COLS), lambda i, j: (i, j, 0)),
        ],
        out_shape=[
            jax.ShapeDtypeStruct((b, s, d), F32),
            jax.ShapeDtypeStruct((b, s, ROUTER_COLS), F32),
        ],
        scratch_shapes=[
            pltpu.VMEM((ts, D_IN_EVEN), F32),
            pltpu.VMEM((CONV_HALO + ts, D_CONV_A), F32),
            pltpu.VMEM((CONV_HALO + ts, D_CONV_B), F32),
            pltpu.VMEM((ts, D_CONV_A + D_CONV_B), BF16),
        ],
        compiler_params=pltpu.CompilerParams(
            dimension_semantics=("arbitrary", "arbitrary"), vmem_limit_bytes=VMEM_LIMIT),
    )(x, w_in.astype(BF16), conv_a, conv_b_w, row(conv_b_bias), row(norm_b_g), row(norm_b_b),
      w_out.astype(BF16), row(ln_g), row(ln_b), rw)


def _qkv_kernel(x_ref, w_ref, o_ref):
    u = jnp.dot(x_ref[0].astype(BF16), w_ref[...], preferred_element_type=F32)
    o_ref[0, :, :D_ATTN] = (u[:, :D_ATTN] * (HEAD_DIM ** -0.5)).astype(BF16)
    o_ref[0, :, D_ATTN:] = u[:, D_ATTN:].astype(BF16)


def _qkv_proj(x, w_qkv):
    b, s, d = x.shape
    ts = min(ROW_TILE, s)
    return pl.pallas_call(
        _qkv_kernel,
        grid=(b, s // ts),
        in_specs=[pl.BlockSpec((1, ts, d), lambda i, j: (i, j, 0)), _const_spec((d, 3 * D_ATTN))],
        out_specs=pl.BlockSpec((1, ts, 3 * D_ATTN), lambda i, j: (i, j, 0)),
        out_shape=jax.ShapeDtypeStruct((b, s, 3 * D_ATTN), BF16),
        compiler_params=pltpu.CompilerParams(
            dimension_semantics=("arbitrary", "arbitrary"), vmem_limit_bytes=VMEM_LIMIT),
    )(x, w_qkv.astype(BF16))


def _attn_kernel(q_ref, k_ref, v_ref, o_ref, acc_ref, carry_ref, *, tq, tk):
    s_len = q_ref.shape[1]
    per_tile = LANES // HEAD_DIM
    lane = lax.broadcasted_iota(jnp.int32, (1, LANES), 1)
    tri = (lax.broadcasted_iota(jnp.int32, (tk, tk), 0)
           >= lax.broadcasted_iota(jnp.int32, (tk, tk), 1)).astype(BF16)
    causal = (lax.broadcasted_iota(jnp.int32, (tq, tk), 1)
              < lax.broadcasted_iota(jnp.int32, (tq, tk), 0))

    def tile_of(h):
        return slice((h // per_tile) * LANES, (h // per_tile + 1) * LANES)

    for q0 in range(0, s_len, tq):
        qms = [jnp.where((lane // HEAD_DIM) == h % per_tile, q_ref[0, q0:q0 + tq, tile_of(h)],
                         jnp.zeros((), BF16)) for h in range(ATTN_HEADS)]
        acc_ref[...] = jnp.zeros_like(acc_ref)
        carry_ref[...] = jnp.zeros_like(carry_ref)

        def block(k0, diagonal):
            for h in range(ATTN_HEADS):
                kb = k_ref[0, pl.ds(k0, tk), tile_of(h)]
                vb = v_ref[0, pl.ds(k0, tk), tile_of(h)]
                z = lax.dot_general(qms[h], kb, (((1,), (1,)), ((), ())),
                                    preferred_element_type=F32)
                sp = jnp.maximum(z, 0.0) + jnp.log(1.0 + jnp.exp(-jnp.abs(z)))
                if diagonal:
                    sp = jnp.where(causal, sp, 0.0)
                c = jnp.dot(sp.astype(BF16), tri, preferred_element_type=F32)
                carry = carry_ref[h]
                a = jnp.exp(z - c - jnp.concatenate([carry] * (tk // LANES), axis=1))
                if diagonal:
                    a = jnp.where(causal, a, 0.0)
                acc_ref[h] += jnp.dot(a.astype(BF16), vb, preferred_element_type=F32)
                carry_ref[h] = carry + jnp.broadcast_to(c[:, 0:1], carry.shape)

        block(q0, True)

        def body(i, _):
            block(pl.multiple_of(q0 - tk * (i + 1), tk), False)
            return 0

        if q0 > 0:
            lax.fori_loop(0, q0 // tk, body, 0)
        for t in range(ATTN_HEADS // per_tile):
            o = acc_ref[t * per_tile]
            for h in range(1, per_tile):
                o = jnp.where((lane // HEAD_DIM) == h, acc_ref[t * per_tile + h], o)
            o_ref[0, q0:q0 + tq, t * LANES:(t + 1) * LANES] = o.astype(BF16)


def _attention(qkv):
    b, s, _ = qkv.shape
    tq = tk = min(ATTN_BLOCK, s)
    width = ATTN_HEADS * HEAD_DIM
    n_steps = D_ATTN // width
    return pl.pallas_call(
        functools.partial(_attn_kernel, tq=tq, tk=tk),
        grid=(b, n_steps),
        in_specs=[
            pl.BlockSpec((1, s, width), lambda i, p: (i, 0, p)),
            pl.BlockSpec((1, s, width), lambda i, p: (i, 0, n_steps + p)),
            pl.BlockSpec((1, s, width), lambda i, p: (i, 0, 2 * n_steps + p)),
        ],
        out_specs=pl.BlockSpec((1, s, width), lambda i, p: (i, 0, p)),
        out_shape=jax.ShapeDtypeStruct((b, s, D_ATTN), BF16),
        scratch_shapes=[pltpu.VMEM((ATTN_HEADS, tq, LANES), F32),
                        pltpu.VMEM((ATTN_HEADS, tq, LANES), F32)],
        compiler_params=pltpu.CompilerParams(
            dimension_semantics=("arbitrary", "arbitrary"), vmem_limit_bytes=VMEM_LIMIT),
    )(qkv, qkv, qkv)


def _oproj_kernel(o_ref, x_ref, w_ref, lng_ref, lnb_ref, rw_ref, h_ref, logit_ref):
    mix = jnp.dot(o_ref[0], w_ref[...], preferred_element_type=F32)
    _finish(x_ref[0], mix, lng_ref, lnb_ref, rw_ref, h_ref, logit_ref)


def _attn_out(o, x, w_o, ln_g, ln_b, rw):
    b, s, d = x.shape
    ts = min(ROW_TILE, s)
    row = lambda v: v.reshape(1, -1)
    return pl.pallas_call(
        _oproj_kernel,
        grid=(b, s // ts),
        in_specs=[
            pl.BlockSpec((1, ts, D_ATTN), lambda i, j: (i, j, 0)),
            pl.BlockSpec((1, ts, d), lambda i, j: (i, j, 0)),
            _const_spec((D_ATTN, d)),
            _const_spec((1, d)),
            _const_spec((1, d)),
            _const_spec((d, 2 * ROUTER_COLS)),
        ],
        out_specs=[
            pl.BlockSpec((1, ts, d), lambda i, j: (i, j, 0)),
            pl.BlockSpec((1, ts, ROUTER_COLS), lambda i, j: (i, j, 0)),
        ],
        out_shape=[
            jax.ShapeDtypeStruct((b, s, d), F32),
            jax.ShapeDtypeStruct((b, s, ROUTER_COLS), F32),
        ],
        compiler_params=pltpu.CompilerParams(
            dimension_semantics=("arbitrary", "arbitrary"), vmem_limit_bytes=VMEM_LIMIT),
    )(o, x, w_o.astype(BF16), row(ln_g), row(ln_b), rw)


def _routing(logits, rg_b, re_b, chunk):
    t = logits.shape[0]
    g_logits = logits[:, :N_GROUPS] + rg_b
    grp = jnp.argmax(g_logits, axis=-1)
    g_gate = jnp.take_along_axis(jax.nn.softmax(g_logits, axis=-1), grp[:, None], axis=-1)
    e_logits = (logits[:, N_GROUPS:N_GROUPS + N_EXPERTS] + re_b).reshape(t, N_GROUPS, EXPERTS_PER_GROUP)
    e_logits = jnp.take_along_axis(e_logits, grp[:, None, None], axis=1)[:, 0]
    top_val, top_idx = lax.top_k(e_logits, TOP_K)
    weights = g_gate * jax.nn.softmax(top_val, axis=-1)
    ids = (grp[:, None] * EXPERTS_PER_GROUP + top_idx).reshape(-1).astype(jnp.int32)

    n_assign = t * TOP_K
    onehot = (ids[:, None] == jnp.arange(N_EXPERTS, dtype=jnp.int32)[None, :]).astype(jnp.int32)
    csum = jnp.cumsum(onehot, axis=0)
    sizes = csum[-1]
    rank = jnp.take_along_axis(csum, ids[:, None], axis=1)[:, 0] - 1
    padded = (sizes + chunk - 1) // chunk * chunk
    pad_end = jnp.cumsum(padded)
    pad_start = pad_end - padded
    dest = (pad_start[ids] + rank).astype(jnp.int32)
    n_chunks = -(-n_assign // chunk) + N_EXPERTS
    tok = jnp.arange(n_assign, dtype=jnp.int32) // TOP_K
    row_tok = jnp.zeros((n_chunks * chunk,), jnp.int32).at[dest].set(tok)
    chunk_expert = jnp.minimum(
        jnp.searchsorted(pad_end, jnp.arange(n_chunks, dtype=jnp.int32) * chunk, side="right"),
        N_EXPERTS - 1).astype(jnp.int32)
    n_valid = (pad_end[-1] // chunk).astype(jnp.int32).reshape(1)
    return weights.astype(F32), dest, row_tok, chunk_expert, n_valid, n_chunks


def _sc_row_gather(x, idx):
    _, d = x.shape
    m = idx.shape[0]
    window = SC_GATHER_ROWS
    mesh = plsc.VectorSubcoreMesh(core_axis_name="core", subcore_axis_name="subcore")
    idx_rows = jnp.pad(idx.reshape(m // window, window), ((0, 0), (0, LANES - window)))

    @pl.kernel(out_type=jax.ShapeDtypeStruct((m, d), x.dtype), mesh=mesh, scratch_types=[])
    def gather(x_hbm, i_hbm, o_hbm):
        def body(i_vmem, o_vmem):
            pltpu.sync_copy(x_hbm.at[i_vmem.at[0, pl.ds(0, window)]], o_vmem)

        pltpu.emit_pipeline(
            body,
            grid=(m // window,),
            in_specs=[pl.BlockSpec((1, LANES), lambda i: (i, 0))],
            out_specs=[pl.BlockSpec((window, d), lambda i: (i, 0))],
            core_axis_name=("core", "subcore"),
            dimension_semantics=(pltpu.PARALLEL,),
        )(i_hbm, o_hbm)

    return gather(x, idx_rows)


def _expert_kernel(ce_ref, nv_ref, x_ref, wup_ref, wdn_ref, y_ref, wup_bf, wdn_bf):
    i = pl.program_id(0)

    @pl.when(i < nv_ref[0])
    def _():
        prev = ce_ref[jnp.maximum(i - 1, 0)]

        @pl.when((i == 0) | (ce_ref[i] != prev))
        def _():
            wup_bf[...] = wup_ref[0, 0].astype(BF16)
            wdn_bf[...] = wdn_ref[0, 0].astype(BF16)

        gu = jnp.dot(x_ref[...].astype(BF16), wup_bf[...], preferred_element_type=F32)
        gate, up = gu[:, :D_EXPERT], gu[:, D_EXPERT:]
        act = gate * jax.nn.sigmoid(gate) * up
        y_ref[...] = jnp.dot(act.astype(BF16), wdn_bf[...], preferred_element_type=F32)

    @pl.when(i >= nv_ref[0])
    def _():
        y_ref[...] = jnp.zeros_like(y_ref)


def _experts(x_rows, chunk_expert, n_valid, n_chunks, chunk, layer, w_up, w_down):
    _, d = x_rows.shape
    grid_spec = pltpu.PrefetchScalarGridSpec(
        num_scalar_prefetch=2,
        grid=(n_chunks,),
        in_specs=[
            pl.BlockSpec((chunk, d), lambda i, ce, nv: (jnp.minimum(i, nv[0] - 1), 0)),
            pl.BlockSpec((1, 1, d, 2 * D_EXPERT), lambda i, ce, nv: (layer, ce[i], 0, 0)),
            pl.BlockSpec((1, 1, D_EXPERT, d), lambda i, ce, nv: (layer, ce[i], 0, 0)),
        ],
        out_specs=pl.BlockSpec((chunk, d), lambda i, ce, nv: (i, 0)),
        scratch_shapes=[
            pltpu.VMEM((d, 2 * D_EXPERT), BF16),
            pltpu.VMEM((D_EXPERT, d), BF16),
        ],
    )
    return pl.pallas_call(
        _expert_kernel,
        grid_spec=grid_spec,
        out_shape=jax.ShapeDtypeStruct((n_chunks * chunk, d), F32),
        compiler_params=pltpu.CompilerParams(
            dimension_semantics=("arbitrary",), vmem_limit_bytes=VMEM_LIMIT),
    )(chunk_expert, n_valid, x_rows, w_up, w_down)


def _combine_kernel(h_ref, w_ref, *rest):
    y_refs, (lng_ref, lnb_ref, o_ref) = rest[:TOP_K], rest[TOP_K:]
    w = w_ref[...]
    ffn = w[:, 0:1] * y_refs[0][...]
    for k in range(1, TOP_K):
        ffn = ffn + w[:, k:k + 1] * y_refs[k][...]
    o_ref[...] = _layer_norm(DEEPNORM_ALPHA * h_ref[...] + ffn, lng_ref[...], lnb_ref[...])


def _combine(h2d, weights, y_sel, ln_g, ln_b):
    t, d = h2d.shape
    tm = min(ROW_TILE, t)
    row = lambda v: v.reshape(1, -1)
    y_specs = [pl.BlockSpec((tm, d), functools.partial(lambda i, k: (k * (t // tm) + i, 0), k=k))
               for k in range(TOP_K)]
    return pl.pallas_call(
        _combine_kernel,
        grid=(t // tm,),
        in_specs=[pl.BlockSpec((tm, d), lambda i: (i, 0)),
                  pl.BlockSpec((tm, TOP_K), lambda i: (i, 0)),
                  *y_specs, _const_spec((1, d)), _const_spec((1, d))],
        out_specs=pl.BlockSpec((tm, d), lambda i: (i, 0)),
        out_shape=jax.ShapeDtypeStruct((t, d), F32),
        compiler_params=pltpu.CompilerParams(
            dimension_semantics=("arbitrary",), vmem_limit_bytes=VMEM_LIMIT),
    )(h2d, weights, *([y_sel] * TOP_K), row(ln_g), row(ln_b))


def _moe_block(h, logits, rg_b, re_b, layer, w_up, w_down, ln_g, ln_b):
    b, s, d = h.shape
    t = b * s
    chunk = min(EXPERT_CHUNK, t)
    weights, dest, row_tok, chunk_expert, n_valid, n_chunks = _routing(
        logits.reshape(t, ROUTER_COLS), rg_b, re_b, chunk)
    h2d = h.reshape(t, d)
    x_rows = _sc_row_gather(h2d, row_tok)
    y_rows = _experts(x_rows, chunk_expert, n_valid, n_chunks, chunk, layer, w_up, w_down)
    y_sel = _sc_row_gather(y_rows, dest.reshape(t, TOP_K).T.reshape(-1))
    return _combine(h2d, weights, y_sel, ln_g, ln_b).reshape(b, s, d)


def _router_weights(rg_w, re_w):
    w = jnp.concatenate([rg_w, re_w], axis=1)
    w = jnp.pad(w, ((0, 0), (0, ROUTER_COLS - w.shape[1])))
    hi = w.astype(BF16)
    lo = (w - hi.astype(F32)).astype(BF16)
    return jnp.concatenate([hi, lo], axis=1)


def kernel(x, even_w_in, even_conv_a, even_conv_b_w, even_conv_b_bias, even_norm_b_g,
           even_norm_b_b, even_w_out, odd_w_qkv, odd_w_o, ln_mix_g, ln_mix_b, ln_ffn_g,
           ln_ffn_b, router_group_w, router_group_b, router_expert_w, router_expert_b,
           expert_w_up, expert_w_down):
    for layer in range(DEPTH):
        i = layer // 2
        rw = _router_weights(router_group_w[layer], router_expert_w[layer])
        if layer % 2 == 0:
            h, logits = _even_mixer(x, even_w_in[i], even_conv_a[i], even_conv_b_w[i],
                                    even_conv_b_bias[i], even_norm_b_g[i], even_norm_b_b[i],
                                    even_w_out[i], ln_mix_g[layer], ln_mix_b[layer], rw)
        else:
            o = _attention(_qkv_proj(x, odd_w_qkv[i]))
            h, logits = _attn_out(o, x, odd_w_o[i], ln_mix_g[layer], ln_mix_b[layer], rw)
        x = _moe_block(h, logits, router_group_b[layer], router_expert_b[layer], layer,
                       expert_w_up, expert_w_down, ln_ffn_g[layer], ln_ffn_b[layer])
    return x
```

```python
import functools

import jax
import jax.numpy as jnp
from jax import lax
from jax.experimental import pallas as pl
from jax.experimental.pallas import tpu as pltpu
from jax.experimental.pallas import tpu_sc as plsc

F32 = jnp.float32
BF16 = jnp.bfloat16
U32 = jnp.uint32
I32 = jnp.int32

D_MODEL = 1024
DEPTH = 4
D_CONV_A = 512
CONV_A_WIDTH = 3
D_CONV_B = 512
CONV_B_WIDTH = 31
N_HEADS = 16
HEAD_DIM = 64
D_ATTN = N_HEADS * HEAD_DIM
N_GROUPS = 4
EXPERTS_PER_GROUP = 8
N_EXPERTS = N_GROUPS * EXPERTS_PER_GROUP
TOP_K = 2
D_EXPERT = 512
LN_EPS = 1e-5
DEEPNORM_ALPHA = (2 * DEPTH) ** 0.25
D_IN_EVEN = 3 * D_CONV_A + 2 * D_CONV_B

LANES = 128
ROW_TILE = 512
CONV_HALO = 32
CONV_ROWS = 32
ATTN_BLOCK = 256
ATTN_HEADS = 4
EXPERT_CHUNK = 512
SC_WINDOW = 64
ROUTER_COLS = LANES
VMEM_LIMIT = 56 * 1024 * 1024
NEG_LARGE = -1e30

ROUTE_ID, ROUTE_GATE, ROUTE_RANK = 0, TOP_K, 2 * TOP_K


def _layer_norm(v, g, b):
    mu = jnp.mean(v, axis=-1, keepdims=True)
    c = v - mu
    var = jnp.mean(c * c, axis=-1, keepdims=True)
    return c * lax.rsqrt(var + LN_EPS) * g + b


def _pack_bf16_pairs(v):
    half = v.shape[1] // 2
    hi = lax.bitcast_convert_type(v[:, :half].astype(BF16).astype(F32), U32)
    lo = lax.bitcast_convert_type(v[:, half:].astype(BF16).astype(F32), U32)
    return hi | (lo >> 16)


def _unpack_bf16_pairs(w):
    hi = lax.bitcast_convert_type(w & jnp.uint32(0xFFFF0000), F32)
    lo = lax.bitcast_convert_type(w << 16, F32)
    return jnp.concatenate([hi, lo], axis=1)


def _router_logits(h, rw_ref):
    hi = h.astype(BF16)
    lo = (h - hi.astype(F32)).astype(BF16)
    r1 = jnp.dot(hi, rw_ref[...], preferred_element_type=F32)
    r2 = jnp.dot(lo, rw_ref[:, :ROUTER_COLS], preferred_element_type=F32)
    return r1[:, :ROUTER_COLS] + r1[:, ROUTER_COLS:] + r2


def _route(logits, tri_ref, base):
    lane = lax.broadcasted_iota(I32, logits.shape, 1)
    lane_f = lane.astype(F32)

    def top(vals):
        m = jnp.max(vals, axis=1, keepdims=True)
        return m, jnp.min(jnp.where(vals == m, lane_f, float(LANES)), axis=1, keepdims=True)

    is_group = lane < N_GROUPS
    g_max, grp = top(jnp.where(is_group, logits, NEG_LARGE))
    g_gate = 1.0 / jnp.sum(jnp.where(is_group, jnp.exp(logits - g_max), 0.0), axis=1, keepdims=True)
    first = N_GROUPS + EXPERTS_PER_GROUP * grp
    e_logits = jnp.where((lane_f >= first) & (lane_f < first + EXPERTS_PER_GROUP), logits, NEG_LARGE)
    v0, l0 = top(e_logits)
    v1, l1 = top(jnp.where(lane_f == l0, NEG_LARGE, e_logits))
    e = jnp.exp(v1 - v0)
    gate0 = g_gate / (1.0 + e)
    gate1 = gate0 * e

    hot0 = jnp.where(lane_f == l0, 1.0, 0.0)
    hot1 = jnp.where(lane_f == l1, 1.0, 0.0)
    before0 = jnp.dot(tri_ref[...], hot0.astype(BF16), preferred_element_type=F32)
    before1 = jnp.dot(tri_ref[...], hot1.astype(BF16), preferred_element_type=F32)
    n0 = jnp.sum(hot0, axis=0, keepdims=True)
    n1 = jnp.sum(hot1, axis=0, keepdims=True)
    rank0 = jnp.sum(hot0 * (before0 + base), axis=1, keepdims=True)
    rank1 = jnp.sum(hot1 * (before1 + base + n0), axis=1, keepdims=True)

    fields = [l0 - N_GROUPS, l1 - N_GROUPS, gate0, gate1, rank0, rank1]
    record = jnp.zeros(logits.shape, F32)
    for i, f in enumerate(fields):
        record = jnp.where(lane == i, f, record)
    return record, base + n0 + n1


def _finish(x, mix, is_first_step, lng_ref, lnb_ref, rw_ref, rb_ref, tri_ref,
            h_ref, hp_ref, route_ref, counts_ref, cnt_ref):
    h = _layer_norm(DEEPNORM_ALPHA * x + mix, lng_ref[...], lnb_ref[...])
    h_ref[0] = h
    hp_ref[0] = _pack_bf16_pairs(h)

    @pl.when(is_first_step)
    def _():
        cnt_ref[...] = jnp.zeros_like(cnt_ref)

    record, counts = _route(_router_logits(h, rw_ref) + rb_ref[...], tri_ref, cnt_ref[...])
    route_ref[0] = record
    cnt_ref[...] = counts
    counts_ref[...] = counts


def _const_spec(shape):
    return pl.BlockSpec(shape, lambda *_: (0,) * len(shape))


def _finish_specs(b, s, d, ts):
    in_specs = [_const_spec((1, d)), _const_spec((1, d)), _const_spec((d, 2 * ROUTER_COLS)),
                _const_spec((1, ROUTER_COLS)), _const_spec((ts, ts))]
    tile = lambda w: pl.BlockSpec((1, ts, w), lambda i, j: (i, j, 0))
    out_specs = [tile(d), tile(d // 2), tile(ROUTER_COLS), _const_spec((1, ROUTER_COLS))]
    out_shape = [jax.ShapeDtypeStruct((b, s, d), F32), jax.ShapeDtypeStruct((b, s, d // 2), U32),
                 jax.ShapeDtypeStruct((b, s, ROUTER_COLS), F32),
                 jax.ShapeDtypeStruct((1, ROUTER_COLS), F32)]
    return in_specs, out_specs, out_shape, [pltpu.VMEM((1, ROUTER_COLS), F32)]


def _finish_args(ln_g, ln_b, router, ts):
    rw, rb = router
    tri = (lax.broadcasted_iota(I32, (ts, ts), 1) < lax.broadcasted_iota(I32, (ts, ts), 0)).astype(BF16)
    return ln_g.reshape(1, -1), ln_b.reshape(1, -1), rw, rb, tri


def _is_first_step():
    return (pl.program_id(0) == 0) & (pl.program_id(1) == 0)


def _even_kernel(x_ref, win_ref, ca_ref, cbw_ref, cbb_ref, nbg_ref, nbb_ref, wout_ref,
                 lng_ref, lnb_ref, rw_ref, rb_ref, tri_ref,
                 h_ref, hp_ref, route_ref, counts_ref,
                 cnt_ref, u_buf, cv_buf, g_buf, y_buf):
    s = pl.program_id(1)
    ts = x_ref.shape[1]
    x = x_ref[0]
    u_buf[...] = jnp.dot(x.astype(BF16), win_ref[...], preferred_element_type=F32)

    @pl.when(s == 0)
    def _():
        cv_buf[0:CONV_HALO] = jnp.zeros((CONV_HALO, D_CONV_A), F32)
        g_buf[0:CONV_HALO] = jnp.zeros((CONV_HALO, D_CONV_B), F32)

    @pl.when(s > 0)
    def _():
        cv_buf[0:CONV_HALO] = cv_buf[ts:ts + CONV_HALO]
        g_buf[0:CONV_HALO] = g_buf[ts:ts + CONV_HALO]

    c0 = D_CONV_A
    cv_buf[CONV_HALO:CONV_HALO + ts] = u_buf[:, c0:2 * c0] * u_buf[:, 2 * c0:3 * c0]
    b_val = u_buf[:, 3 * c0:3 * c0 + D_CONV_B]
    b_gate = u_buf[:, 3 * c0 + D_CONV_B:]
    g_buf[CONV_HALO:CONV_HALO + ts] = b_val * jax.nn.sigmoid(b_gate)

    for r0 in range(0, ts, CONV_ROWS):
        acc_a = None
        for k in range(CONV_A_WIDTH):
            off = CONV_HALO + r0 - (CONV_A_WIDTH - 1) + k
            term = ca_ref[k:k + 1, :] * cv_buf[off:off + CONV_ROWS, :]
            acc_a = term if acc_a is None else acc_a + term
        y_a = u_buf[r0:r0 + CONV_ROWS, 0:c0] * acc_a
        acc_b = None
        for k in range(CONV_B_WIDTH):
            off = CONV_HALO + r0 - (CONV_B_WIDTH - 1) + k
            term = cbw_ref[k:k + 1, :] * g_buf[off:off + CONV_ROWS, :]
            acc_b = term if acc_b is None else acc_b + term
        gb = _layer_norm(acc_b + cbb_ref[...], nbg_ref[...], nbb_ref[...])
        y_b = gb * jax.nn.sigmoid(gb)
        y_buf[r0:r0 + CONV_ROWS, 0:c0] = y_a.astype(BF16)
        y_buf[r0:r0 + CONV_ROWS, c0:] = y_b.astype(BF16)

    mix = jnp.dot(y_buf[...], wout_ref[...], preferred_element_type=F32)
    _finish(x, mix, _is_first_step(), lng_ref, lnb_ref, rw_ref, rb_ref, tri_ref,
            h_ref, hp_ref, route_ref, counts_ref, cnt_ref)


def _even_mixer(x, w_in, conv_a, conv_b_w, conv_b_bias, norm_b_g, norm_b_b, w_out, ln_g, ln_b, router):
    b, s, d = x.shape
    ts = min(ROW_TILE, s)
    row = lambda v: v.reshape(1, -1)
    f_in, f_out, f_shape, f_scratch = _finish_specs(b, s, d, ts)
    return pl.pallas_call(
        _even_kernel,
        grid=(b, s // ts),
        in_specs=[
            pl.BlockSpec((1, ts, d), lambda i, j: (i, j, 0)),
            _const_spec((d, D_IN_EVEN)),
            _const_spec((CONV_A_WIDTH, D_CONV_A)),
            _const_spec((CONV_B_WIDTH, D_CONV_B)),
            _const_spec((1, D_CONV_B)),
            _const_spec((1, D_CONV_B)),
            _const_spec((1, D_CONV_B)),
            _const_spec((D_CONV_A + D_CONV_B, d)),
            *f_in,
        ],
        out_specs=f_out,
        out_shape=f_shape,
        scratch_shapes=[
            *f_scratch,
            pltpu.VMEM((ts, D_IN_EVEN), F32),
            pltpu.VMEM((CONV_HALO + ts, D_CONV_A), F32),
            pltpu.VMEM((CONV_HALO + ts, D_CONV_B), F32),
            pltpu.VMEM((ts, D_CONV_A + D_CONV_B), BF16),
        ],
        compiler_params=pltpu.CompilerParams(
            dimension_semantics=("arbitrary", "arbitrary"), vmem_limit_bytes=VMEM_LIMIT),
    )(x, w_in.astype(BF16), conv_a, conv_b_w, row(conv_b_bias), row(norm_b_g), row(norm_b_b),
      w_out.astype(BF16), *_finish_args(ln_g, ln_b, router, ts))


def _qkv_kernel(x_ref, w_ref, o_ref):
    u = jnp.dot(x_ref[0].astype(BF16), w_ref[...], preferred_element_type=F32)
    o_ref[0, :, :D_ATTN] = (u[:, :D_ATTN] * (HEAD_DIM ** -0.5)).astype(BF16)
    o_ref[0, :, D_ATTN:] = u[:, D_ATTN:].astype(BF16)


def _qkv_proj(x, w_qkv):
    b, s, d = x.shape
    ts = min(ROW_TILE, s)
    return pl.pallas_call(
        _qkv_kernel,
        grid=(b, s // ts),
        in_specs=[pl.BlockSpec((1, ts, d), lambda i, j: (i, j, 0)), _const_spec((d, 3 * D_ATTN))],
        out_specs=pl.BlockSpec((1, ts, 3 * D_ATTN), lambda i, j: (i, j, 0)),
        out_shape=jax.ShapeDtypeStruct((b, s, 3 * D_ATTN), BF16),
        compiler_params=pltpu.CompilerParams(
            dimension_semantics=("arbitrary", "arbitrary"), vmem_limit_bytes=VMEM_LIMIT),
    )(x, w_qkv.astype(BF16))


def _attn_kernel(q_ref, k_ref, v_ref, o_ref, acc_ref, carry_ref, *, tq, tk):
    s_len = q_ref.shape[1]
    per_tile = LANES // HEAD_DIM
    lane = lax.broadcasted_iota(jnp.int32, (1, LANES), 1)
    tri = (lax.broadcasted_iota(jnp.int32, (tk, tk), 0)
           >= lax.broadcasted_iota(jnp.int32, (tk, tk), 1)).astype(BF16)
    causal = (lax.broadcasted_iota(jnp.int32, (tq, tk), 1)
              < lax.broadcasted_iota(jnp.int32, (tq, tk), 0))

    def tile_of(h):
        return slice((h // per_tile) * LANES, (h // per_tile + 1) * LANES)

    for q0 in range(0, s_len, tq):
        qms = [jnp.where((lane // HEAD_DIM) == h % per_tile, q_ref[0, q0:q0 + tq, tile_of(h)],
                         jnp.zeros((), BF16)) for h in range(ATTN_HEADS)]
        acc_ref[...] = jnp.zeros_like(acc_ref)
        carry_ref[...] = jnp.zeros_like(carry_ref)

        def block(k0, diagonal):
            for h in range(ATTN_HEADS):
                kb = k_ref[0, pl.ds(k0, tk), tile_of(h)]
                vb = v_ref[0, pl.ds(k0, tk), tile_of(h)]
                z = lax.dot_general(qms[h], kb, (((1,), (1,)), ((), ())),
                                    preferred_element_type=F32)
                sp = jnp.maximum(z, 0.0) + jnp.log(1.0 + jnp.exp(-jnp.abs(z)))
                if diagonal:
                    sp = jnp.where(causal, sp, 0.0)
                c = jnp.dot(sp.astype(BF16), tri, preferred_element_type=F32)
                carry = carry_ref[h]
                a = jnp.exp(z - c - jnp.concatenate([carry] * (tk // LANES), axis=1))
                if diagonal:
                    a = jnp.where(causal, a, 0.0)
                acc_ref[h] += jnp.dot(a.astype(BF16), vb, preferred_element_type=F32)
                carry_ref[h] = carry + jnp.broadcast_to(c[:, 0:1], carry.shape)

        block(q0, True)

        def body(i, _):
            block(pl.multiple_of(q0 - tk * (i + 1), tk), False)
            return 0

        if q0 > 0:
            lax.fori_loop(0, q0 // tk, body, 0)
        for t in range(ATTN_HEADS // per_tile):
            o = acc_ref[t * per_tile]
            for h in range(1, per_tile):
                o = jnp.where((lane // HEAD_DIM) == h, acc_ref[t * per_tile + h], o)
            o_ref[0, q0:q0 + tq, t * LANES:(t + 1) * LANES] = o.astype(BF16)


def _attention(qkv):
    b, s, _ = qkv.shape
    tq = tk = min(ATTN_BLOCK, s)
    width = ATTN_HEADS * HEAD_DIM
    n_steps = D_ATTN // width
    return pl.pallas_call(
        functools.partial(_attn_kernel, tq=tq, tk=tk),
        grid=(b, n_steps),
        in_specs=[
            pl.BlockSpec((1, s, width), lambda i, p: (i, 0, p)),
            pl.BlockSpec((1, s, width), lambda i, p: (i, 0, n_steps + p)),
            pl.BlockSpec((1, s, width), lambda i, p: (i, 0, 2 * n_steps + p)),
        ],
        out_specs=pl.BlockSpec((1, s, width), lambda i, p: (i, 0, p)),
        out_shape=jax.ShapeDtypeStruct((b, s, D_ATTN), BF16),
        scratch_shapes=[pltpu.VMEM((ATTN_HEADS, tq, LANES), F32),
                        pltpu.VMEM((ATTN_HEADS, tq, LANES), F32)],
        compiler_params=pltpu.CompilerParams(
            dimension_semantics=("arbitrary", "arbitrary"), vmem_limit_bytes=VMEM_LIMIT),
    )(qkv, qkv, qkv)


def _oproj_kernel(o_ref, x_ref, w_ref, lng_ref, lnb_ref, rw_ref, rb_ref, tri_ref,
                  h_ref, hp_ref, route_ref, counts_ref, cnt_ref):
    mix = jnp.dot(o_ref[0], w_ref[...], preferred_element_type=F32)
    _finish(x_ref[0], mix, _is_first_step(), lng_ref, lnb_ref, rw_ref, rb_ref, tri_ref,
            h_ref, hp_ref, route_ref, counts_ref, cnt_ref)


def _attn_out(o, x, w_o, ln_g, ln_b, router):
    b, s, d = x.shape
    ts = min(ROW_TILE, s)
    f_in, f_out, f_shape, f_scratch = _finish_specs(b, s, d, ts)
    return pl.pallas_call(
        _oproj_kernel,
        grid=(b, s // ts),
        in_specs=[
            pl.BlockSpec((1, ts, D_ATTN), lambda i, j: (i, j, 0)),
            pl.BlockSpec((1, ts, d), lambda i, j: (i, j, 0)),
            _const_spec((D_ATTN, d)),
            *f_in,
        ],
        out_specs=f_out,
        out_shape=f_shape,
        scratch_shapes=f_scratch,
        compiler_params=pltpu.CompilerParams(
            dimension_semantics=("arbitrary", "arbitrary"), vmem_limit_bytes=VMEM_LIMIT),
    )(o, x, w_o.astype(BF16), *_finish_args(ln_g, ln_b, router, ts))


def _dispatch_meta(route, counts, chunk):
    t = route.shape[0]
    sizes = counts[0, N_GROUPS:N_GROUPS + N_EXPERTS].astype(I32)
    ids = route[:, ROUTE_ID:ROUTE_ID + TOP_K].astype(I32)
    gates = route[:, ROUTE_GATE:ROUTE_GATE + TOP_K]
    rank = route[:, ROUTE_RANK:ROUTE_RANK + TOP_K].astype(I32)
    padded = (sizes + chunk - 1) // chunk * chunk
    pad_end = jnp.cumsum(padded)
    pad_start = pad_end - padded
    dest = pad_start[ids] + rank
    n_chunks = -(-(t * TOP_K) // chunk) + N_EXPERTS
    chunk_first_row = jnp.arange(n_chunks, dtype=I32) * chunk
    chunk_expert = jnp.minimum(
        jnp.searchsorted(pad_end, chunk_first_row, side="right"), N_EXPERTS - 1).astype(I32)
    chunk_rows = jnp.clip(
        pad_start[chunk_expert] + sizes[chunk_expert] - chunk_first_row, 0, chunk).astype(I32)
    n_valid = (pad_end[-1] // chunk).astype(I32).reshape(1)
    return gates, dest.astype(I32), chunk_expert, chunk_rows, n_valid, n_chunks


def _index_windows(idx):
    return jnp.pad(idx.reshape(-1, SC_WINDOW), ((0, 0), (0, LANES - SC_WINDOW)))


def _sc_row_scatter(x, dest, n_out):
    t, d = x.shape
    mesh = plsc.VectorSubcoreMesh(core_axis_name="core", subcore_axis_name="subcore")
    idx = [_index_windows(dest[:, k]) for k in range(TOP_K)]

    @pl.kernel(out_type=jax.ShapeDtypeStruct((n_out, d), x.dtype), mesh=mesh, scratch_types=[])
    def scatter(x_hbm, *rest):
        i_hbm, o_hbm = rest[:TOP_K], rest[TOP_K]

        def body(x_vmem, *i_vmem):
            for k in range(TOP_K):
                pltpu.sync_copy(x_vmem, o_hbm.at[i_vmem[k].at[0, pl.ds(0, SC_WINDOW)]])

        pltpu.emit_pipeline(
            body,
            grid=(t // SC_WINDOW,),
            in_specs=[pl.BlockSpec((SC_WINDOW, d), lambda i: (i, 0))]
            + [pl.BlockSpec((1, LANES), lambda i: (i, 0))] * TOP_K,
            out_specs=[],
            core_axis_name=("core", "subcore"),
            dimension_semantics=(pltpu.PARALLEL,),
        )(x_hbm, *i_hbm)

    return scatter(x, *idx)


def _sc_row_gather(x, idx):
    _, d = x.shape
    m = idx.shape[0]
    mesh = plsc.VectorSubcoreMesh(core_axis_name="core", subcore_axis_name="subcore")

    @pl.kernel(out_type=jax.ShapeDtypeStruct((m, d), x.dtype), mesh=mesh, scratch_types=[])
    def gather(x_hbm, i_hbm, o_hbm):
        def body(i_vmem, o_vmem):
            pltpu.sync_copy(x_hbm.at[i_vmem.at[0, pl.ds(0, SC_WINDOW)]], o_vmem)

        pltpu.emit_pipeline(
            body,
            grid=(m // SC_WINDOW,),
            in_specs=[pl.BlockSpec((1, LANES), lambda i: (i, 0))],
            out_specs=[pl.BlockSpec((SC_WINDOW, d), lambda i: (i, 0))],
            core_axis_name=("core", "subcore"),
            dimension_semantics=(pltpu.PARALLEL,),
        )(i_hbm, o_hbm)

    return gather(x, _index_windows(idx))


def _expert_kernel(ce_ref, rows_ref, nv_ref, x_ref, wup_ref, wdn_ref, y_ref, wup_bf, wdn_bf):
    i = pl.program_id(0)

    @pl.when(i < nv_ref[0])
    def _():
        prev = ce_ref[jnp.maximum(i - 1, 0)]

        @pl.when((i == 0) | (ce_ref[i] != prev))
        def _():
            wup_bf[...] = wup_ref[0, 0].astype(BF16)
            wdn_bf[...] = wdn_ref[0, 0].astype(BF16)

        row = lax.broadcasted_iota(I32, x_ref.shape, 0)
        packed = jnp.where(row < rows_ref[i], x_ref[...], jnp.uint32(0))
        gu = jnp.dot(_unpack_bf16_pairs(packed).astype(BF16), wup_bf[...], preferred_element_type=F32)
        gate, up = gu[:, :D_EXPERT], gu[:, D_EXPERT:]
        act = gate * jax.nn.sigmoid(gate) * up
        y = jnp.dot(act.astype(BF16), wdn_bf[...], preferred_element_type=F32)
        y_ref[...] = _pack_bf16_pairs(y)

    @pl.when(i >= nv_ref[0])
    def _():
        y_ref[...] = jnp.zeros_like(y_ref)


def _experts(x_rows, chunk_expert, chunk_rows, n_valid, n_chunks, chunk, layer, w_up, w_down):
    _, half = x_rows.shape
    d = 2 * half
    grid_spec = pltpu.PrefetchScalarGridSpec(
        num_scalar_prefetch=3,
        grid=(n_chunks,),
        in_specs=[
            pl.BlockSpec((chunk, half), lambda i, ce, rows, nv: (jnp.minimum(i, nv[0] - 1), 0)),
            pl.BlockSpec((1, 1, d, 2 * D_EXPERT), lambda i, ce, rows, nv: (layer, ce[i], 0, 0)),
            pl.BlockSpec((1, 1, D_EXPERT, d), lambda i, ce, rows, nv: (layer, ce[i], 0, 0)),
        ],
        out_specs=pl.BlockSpec((chunk, half), lambda i, ce, rows, nv: (i, 0)),
        scratch_shapes=[
            pltpu.VMEM((d, 2 * D_EXPERT), BF16),
            pltpu.VMEM((D_EXPERT, d), BF16),
        ],
    )
    return pl.pallas_call(
        _expert_kernel,
        grid_spec=grid_spec,
        out_shape=jax.ShapeDtypeStruct((n_chunks * chunk, half), U32),
        compiler_params=pltpu.CompilerParams(
            dimension_semantics=("arbitrary",), vmem_limit_bytes=VMEM_LIMIT),
    )(chunk_expert, chunk_rows, n_valid, x_rows, w_up, w_down)


def _combine_kernel(h_ref, g_ref, *rest):
    y_refs, (lng_ref, lnb_ref, o_ref) = rest[:TOP_K], rest[TOP_K:]
    g = g_ref[...]
    ffn = g[:, 0:1] * _unpack_bf16_pairs(y_refs[0][...])
    for k in range(1, TOP_K):
        ffn = ffn + g[:, k:k + 1] * _unpack_bf16_pairs(y_refs[k][...])
    o_ref[...] = _layer_norm(DEEPNORM_ALPHA * h_ref[...] + ffn, lng_ref[...], lnb_ref[...])


def _combine(h2d, gates, y_sel, ln_g, ln_b):
    t, d = h2d.shape
    tm = min(ROW_TILE, t)
    row = lambda v: v.reshape(1, -1)
    y_specs = [pl.BlockSpec((tm, d // 2), functools.partial(lambda i, k: (k * (t // tm) + i, 0), k=k))
               for k in range(TOP_K)]
    return pl.pallas_call(
        _combine_kernel,
        grid=(t // tm,),
        in_specs=[pl.BlockSpec((tm, d), lambda i: (i, 0)),
                  pl.BlockSpec((tm, TOP_K), lambda i: (i, 0)),
                  *y_specs, _const_spec((1, d)), _const_spec((1, d))],
        out_specs=pl.BlockSpec((tm, d), lambda i: (i, 0)),
        out_shape=jax.ShapeDtypeStruct((t, d), F32),
        compiler_params=pltpu.CompilerParams(
            dimension_semantics=("arbitrary",), vmem_limit_bytes=VMEM_LIMIT),
    )(h2d, gates, *([y_sel] * TOP_K), row(ln_g), row(ln_b))


def _moe_block(h, h_packed, route, counts, layer, w_up, w_down, ln_g, ln_b):
    b, s, d = h.shape
    t = b * s
    chunk = min(EXPERT_CHUNK, t)
    gates, dest, chunk_expert, chunk_rows, n_valid, n_chunks = _dispatch_meta(
        route.reshape(t, ROUTER_COLS), counts, chunk)
    x_rows = _sc_row_scatter(h_packed.reshape(t, d // 2), dest, n_chunks * chunk)
    y_rows = _experts(x_rows, chunk_expert, chunk_rows, n_valid, n_chunks, chunk, layer, w_up, w_down)
    y_sel = _sc_row_gather(y_rows, dest.T.reshape(-1))
    return _combine(h.reshape(t, d), gates, y_sel, ln_g, ln_b).reshape(b, s, d)


def _router_params(rg_w, rg_b, re_w, re_b):
    w = jnp.concatenate([rg_w, re_w], axis=1)
    w = jnp.pad(w, ((0, 0), (0, ROUTER_COLS - w.shape[1])))
    hi = w.astype(BF16)
    lo = (w - hi.astype(F32)).astype(BF16)
    bias = jnp.concatenate([rg_b, re_b])
    bias = jnp.pad(bias, (0, ROUTER_COLS - bias.shape[0])).reshape(1, ROUTER_COLS)
    return jnp.concatenate([hi, lo], axis=1), bias


def kernel(x, even_w_in, even_conv_a, even_conv_b_w, even_conv_b_bias, even_norm_b_g,
           even_norm_b_b, even_w_out, odd_w_qkv, odd_w_o, ln_mix_g, ln_mix_b, ln_ffn_g,
           ln_ffn_b, router_group_w, router_group_b, router_expert_w, router_expert_b,
           expert_w_up, expert_w_down):
    for layer in range(DEPTH):
        i = layer // 2
        router = _router_params(router_group_w[layer], router_group_b[layer],
                                router_expert_w[layer], router_expert_b[layer])
        if layer % 2 == 0:
            h, hp, route, counts = _even_mixer(
                x, even_w_in[i], even_conv_a[i], even_conv_b_w[i], even_conv_b_bias[i],
                even_norm_b_g[i], even_norm_b_b[i], even_w_out[i], ln_mix_g[layer], ln_mix_b[layer],
                router)
        else:
            o = _attention(_qkv_proj(x, odd_w_qkv[i]))
            h, hp, route, counts = _attn_out(o, x, odd_w_o[i], ln_mix_g[layer], ln_mix_b[layer], router)
        x = _moe_block(h, hp, route, counts, layer, expert_w_up, expert_w_down,
                       ln_ffn_g[layer], ln_ffn_b[layer])
    return x
```

```python
import functools

import jax
import jax.numpy as jnp
from jax import lax
from jax.experimental import pallas as pl
from jax.experimental.pallas import tpu as pltpu
from jax.experimental.pallas import tpu_sc as plsc

F32 = jnp.float32
BF16 = jnp.bfloat16
U32 = jnp.uint32
I32 = jnp.int32

D_MODEL = 1024
DEPTH = 4
D_CONV_A = 512
CONV_A_WIDTH = 3
D_CONV_B = 512
CONV_B_WIDTH = 31
N_HEADS = 16
HEAD_DIM = 64
D_ATTN = N_HEADS * HEAD_DIM
N_GROUPS = 4
EXPERTS_PER_GROUP = 8
N_EXPERTS = N_GROUPS * EXPERTS_PER_GROUP
TOP_K = 2
D_EXPERT = 512
LN_EPS = 1e-5
DEEPNORM_ALPHA = (2 * DEPTH) ** 0.25
D_IN_EVEN = 3 * D_CONV_A + 2 * D_CONV_B

LANES = 128
ROW_TILE = 512
CONV_HALO = 32
CONV_ROWS = 32
ATTN_BLOCK = 256
ATTN_HEADS = 4
EXPERT_CHUNK = 512
SC_WINDOW = 64
ROUTER_COLS = LANES
VMEM_LIMIT = 56 * 1024 * 1024
NEG_LARGE = -1e30
SOFTPLUS_LINEAR = 80.0

ROUTE_ID, ROUTE_GATE, ROUTE_RANK = 0, TOP_K, 2 * TOP_K
ROUTE_ROWS = 8


def _layer_norm(v, g, b):
    mu = jnp.mean(v, axis=-1, keepdims=True)
    c = v - mu
    var = jnp.mean(c * c, axis=-1, keepdims=True)
    return c * lax.rsqrt(var + LN_EPS) * g + b


def _pack_bf16_pairs(v):
    half = v.shape[1] // 2
    hi = lax.bitcast_convert_type(v[:, :half].astype(BF16).astype(F32), U32)
    lo = lax.bitcast_convert_type(v[:, half:].astype(BF16).astype(F32), U32)
    return hi | (lo >> 16)


def _unpack_bf16_pairs(w):
    hi = lax.bitcast_convert_type(w & jnp.uint32(0xFFFF0000), F32)
    lo = lax.bitcast_convert_type(w << 16, F32)
    return jnp.concatenate([hi, lo], axis=1)


def _router_logits(h, rw_ref):
    hi = h.astype(BF16)
    lo = (h - hi.astype(F32)).astype(BF16)
    r1 = jnp.dot(hi, rw_ref[...], preferred_element_type=F32)
    r2 = jnp.dot(lo, rw_ref[:, :ROUTER_COLS], preferred_element_type=F32)
    return r1[:, :ROUTER_COLS] + r1[:, ROUTER_COLS:] + r2


def _route(logits, tri_ref, base):
    lane = lax.broadcasted_iota(I32, logits.shape, 1)
    lane_f = lane.astype(F32)

    def top(vals):
        m = jnp.max(vals, axis=1, keepdims=True)
        return m, jnp.min(jnp.where(vals == m, lane_f, float(LANES)), axis=1, keepdims=True)

    is_group = lane < N_GROUPS
    g_max, grp = top(jnp.where(is_group, logits, NEG_LARGE))
    g_gate = 1.0 / jnp.sum(jnp.where(is_group, jnp.exp(logits - g_max), 0.0), axis=1, keepdims=True)
    first = N_GROUPS + EXPERTS_PER_GROUP * grp
    e_logits = jnp.where((lane_f >= first) & (lane_f < first + EXPERTS_PER_GROUP), logits, NEG_LARGE)
    v0, l0 = top(e_logits)
    v1, l1 = top(jnp.where(lane_f == l0, NEG_LARGE, e_logits))
    e = jnp.exp(v1 - v0)
    gate0 = g_gate / (1.0 + e)
    gate1 = gate0 * e

    hot0 = jnp.where(lane_f == l0, 1.0, 0.0)
    hot1 = jnp.where(lane_f == l1, 1.0, 0.0)
    before0 = jnp.dot(tri_ref[...], hot0.astype(BF16), preferred_element_type=F32)
    before1 = jnp.dot(tri_ref[...], hot1.astype(BF16), preferred_element_type=F32)
    n0 = jnp.sum(hot0, axis=0, keepdims=True)
    n1 = jnp.sum(hot1, axis=0, keepdims=True)
    rank0 = jnp.sum(hot0 * (before0 + base), axis=1, keepdims=True)
    rank1 = jnp.sum(hot1 * (before1 + base + n0), axis=1, keepdims=True)

    fields = [l0 - N_GROUPS, l1 - N_GROUPS, gate0, gate1, rank0, rank1]
    record = jnp.zeros(logits.shape, F32)
    for i, f in enumerate(fields):
        record = jnp.where(lane == i, f, record)
    return record, base + n0 + n1


def _finish(x, mix, is_first_step, lng_ref, lnb_ref, rw_ref, rb_ref, tri_ref,
            h_ref, hp_ref, route_ref, route_t_ref, counts_ref, cnt_ref):
    h = _layer_norm(DEEPNORM_ALPHA * x + mix, lng_ref[...], lnb_ref[...])
    h_ref[0] = h
    hp_ref[0] = _pack_bf16_pairs(h)

    @pl.when(is_first_step)
    def _():
        cnt_ref[...] = jnp.zeros_like(cnt_ref)

    record, counts = _route(_router_logits(h, rw_ref) + rb_ref[...], tri_ref, cnt_ref[...])
    route_ref[0] = record
    route_t_ref[...] = record.T[:ROUTE_ROWS]
    cnt_ref[...] = counts
    counts_ref[...] = counts


def _const_spec(shape):
    return pl.BlockSpec(shape, lambda *_: (0,) * len(shape))


def _finish_specs(b, s, d, ts):
    in_specs = [_const_spec((1, d)), _const_spec((1, d)), _const_spec((d, 2 * ROUTER_COLS)),
                _const_spec((1, ROUTER_COLS)), _const_spec((ts, ts))]
    tile = lambda w: pl.BlockSpec((1, ts, w), lambda i, j: (i, j, 0))
    out_specs = [tile(d), tile(d // 2), tile(ROUTER_COLS),
                 pl.BlockSpec((ROUTE_ROWS, ts), lambda i, j: (0, i * (s // ts) + j)),
                 _const_spec((1, ROUTER_COLS))]
    out_shape = [jax.ShapeDtypeStruct((b, s, d), F32), jax.ShapeDtypeStruct((b, s, d // 2), U32),
                 jax.ShapeDtypeStruct((b, s, ROUTER_COLS), F32),
                 jax.ShapeDtypeStruct((ROUTE_ROWS, b * s), F32),
                 jax.ShapeDtypeStruct((1, ROUTER_COLS), F32)]
    return in_specs, out_specs, out_shape, [pltpu.VMEM((1, ROUTER_COLS), F32)]


def _finish_args(ln_g, ln_b, router, ts):
    rw, rb = router
    tri = (lax.broadcasted_iota(I32, (ts, ts), 1) < lax.broadcasted_iota(I32, (ts, ts), 0)).astype(BF16)
    return ln_g.reshape(1, -1), ln_b.reshape(1, -1), rw, rb, tri


def _is_first_step():
    return (pl.program_id(0) == 0) & (pl.program_id(1) == 0)


def _even_kernel(x_ref, win_ref, ca_ref, cbw_ref, cbb_ref, nbg_ref, nbb_ref, wout_ref,
                 lng_ref, lnb_ref, rw_ref, rb_ref, tri_ref,
                 h_ref, hp_ref, route_ref, route_t_ref, counts_ref,
                 cnt_ref, u_buf, cv_buf, g_buf, y_buf):
    s = pl.program_id(1)
    ts = x_ref.shape[1]
    x = x_ref[0]
    u_buf[...] = jnp.dot(x.astype(BF16), win_ref[...], preferred_element_type=F32)

    @pl.when(s == 0)
    def _():
        cv_buf[0:CONV_HALO] = jnp.zeros((CONV_HALO, D_CONV_A), F32)
        g_buf[0:CONV_HALO] = jnp.zeros((CONV_HALO, D_CONV_B), F32)

    @pl.when(s > 0)
    def _():
        cv_buf[0:CONV_HALO] = cv_buf[ts:ts + CONV_HALO]
        g_buf[0:CONV_HALO] = g_buf[ts:ts + CONV_HALO]

    c0 = D_CONV_A
    cv_buf[CONV_HALO:CONV_HALO + ts] = u_buf[:, c0:2 * c0] * u_buf[:, 2 * c0:3 * c0]
    b_val = u_buf[:, 3 * c0:3 * c0 + D_CONV_B]
    b_gate = u_buf[:, 3 * c0 + D_CONV_B:]
    g_buf[CONV_HALO:CONV_HALO + ts] = b_val * jax.nn.sigmoid(b_gate)

    for r0 in range(0, ts, CONV_ROWS):
        acc_a = None
        for k in range(CONV_A_WIDTH):
            off = CONV_HALO + r0 - (CONV_A_WIDTH - 1) + k
            term = ca_ref[k:k + 1, :] * cv_buf[off:off + CONV_ROWS, :]
            acc_a = term if acc_a is None else acc_a + term
        y_a = u_buf[r0:r0 + CONV_ROWS, 0:c0] * acc_a
        acc_b = None
        for k in range(CONV_B_WIDTH):
            off = CONV_HALO + r0 - (CONV_B_WIDTH - 1) + k
            term = cbw_ref[k:k + 1, :] * g_buf[off:off + CONV_ROWS, :]
            acc_b = term if acc_b is None else acc_b + term
        gb = _layer_norm(acc_b + cbb_ref[...], nbg_ref[...], nbb_ref[...])
        y_b = gb * jax.nn.sigmoid(gb)
        y_buf[r0:r0 + CONV_ROWS, 0:c0] = y_a.astype(BF16)
        y_buf[r0:r0 + CONV_ROWS, c0:] = y_b.astype(BF16)

    mix = jnp.dot(y_buf[...], wout_ref[...], preferred_element_type=F32)
    _finish(x, mix, _is_first_step(), lng_ref, lnb_ref, rw_ref, rb_ref, tri_ref,
            h_ref, hp_ref, route_ref, route_t_ref, counts_ref, cnt_ref)


def _even_mixer(x, w_in, conv_a, conv_b_w, conv_b_bias, norm_b_g, norm_b_b, w_out, ln_g, ln_b, router):
    b, s, d = x.shape
    ts = min(ROW_TILE, s)
    row = lambda v: v.reshape(1, -1)
    f_in, f_out, f_shape, f_scratch = _finish_specs(b, s, d, ts)
    return pl.pallas_call(
        _even_kernel,
        grid=(b, s // ts),
        in_specs=[
            pl.BlockSpec((1, ts, d), lambda i, j: (i, j, 0)),
            _const_spec((d, D_IN_EVEN)),
            _const_spec((CONV_A_WIDTH, D_CONV_A)),
            _const_spec((CONV_B_WIDTH, D_CONV_B)),
            _const_spec((1, D_CONV_B)),
            _const_spec((1, D_CONV_B)),
            _const_spec((1, D_CONV_B)),
            _const_spec((D_CONV_A + D_CONV_B, d)),
            *f_in,
        ],
        out_specs=f_out,
        out_shape=f_shape,
        scratch_shapes=[
            *f_scratch,
            pltpu.VMEM((ts, D_IN_EVEN), F32),
            pltpu.VMEM((CONV_HALO + ts, D_CONV_A), F32),
            pltpu.VMEM((CONV_HALO + ts, D_CONV_B), F32),
            pltpu.VMEM((ts, D_CONV_A + D_CONV_B), BF16),
        ],
        compiler_params=pltpu.CompilerParams(
            dimension_semantics=("arbitrary", "arbitrary"), vmem_limit_bytes=VMEM_LIMIT),
    )(x, w_in.astype(BF16), conv_a, conv_b_w, row(conv_b_bias), row(norm_b_g), row(norm_b_b),
      w_out.astype(BF16), *_finish_args(ln_g, ln_b, router, ts))


def _qkv_kernel(x_ref, w_ref, o_ref):
    u = jnp.dot(x_ref[0].astype(BF16), w_ref[...], preferred_element_type=F32)
    o_ref[0, :, :D_ATTN] = (u[:, :D_ATTN] * (HEAD_DIM ** -0.5)).astype(BF16)
    o_ref[0, :, D_ATTN:] = u[:, D_ATTN:].astype(BF16)


def _qkv_proj(x, w_qkv):
    b, s, d = x.shape
    ts = min(ROW_TILE, s)
    return pl.pallas_call(
        _qkv_kernel,
        grid=(b, s // ts),
        in_specs=[pl.BlockSpec((1, ts, d), lambda i, j: (i, j, 0)), _const_spec((d, 3 * D_ATTN))],
        out_specs=pl.BlockSpec((1, ts, 3 * D_ATTN), lambda i, j: (i, j, 0)),
        out_shape=jax.ShapeDtypeStruct((b, s, 3 * D_ATTN), BF16),
        compiler_params=pltpu.CompilerParams(
            dimension_semantics=("arbitrary", "arbitrary"), vmem_limit_bytes=VMEM_LIMIT),
    )(x, w_qkv.astype(BF16))


def _attn_kernel(q_ref, k_ref, v_ref, o_ref, acc_ref, carry_ref, z_buf, sp_buf, *, blk):
    s_len = q_ref.shape[1]
    per_tile = LANES // HEAD_DIM
    lane = lax.broadcasted_iota(jnp.int32, (1, LANES), 1)
    tri = (lax.broadcasted_iota(jnp.int32, (blk, blk), 0)
           >= lax.broadcasted_iota(jnp.int32, (blk, blk), 1)).astype(BF16)
    causal = (lax.broadcasted_iota(jnp.int32, (blk, blk), 1)
              < lax.broadcasted_iota(jnp.int32, (blk, blk), 0))

    def tile_of(h):
        return slice((h // per_tile) * LANES, (h // per_tile + 1) * LANES)

    def q_block(n_back, _):
        q0 = pl.multiple_of(n_back * blk, blk)
        qms = [jnp.where((lane // HEAD_DIM) == h % per_tile, q_ref[0, pl.ds(q0, blk), tile_of(h)],
                         jnp.zeros((), BF16)) for h in range(ATTN_HEADS)]
        acc_ref[...] = jnp.zeros_like(acc_ref)
        carry_ref[...] = jnp.zeros_like(carry_ref)

        def score(m, slot, diagonal):
            k0 = pl.multiple_of(q0 - m * blk, blk)
            for h in range(ATTN_HEADS):
                kb = k_ref[0, pl.ds(k0, blk), tile_of(h)]
                z = lax.dot_general(qms[h], kb, (((1,), (1,)), ((), ())),
                                    preferred_element_type=F32)
                sp = jnp.where(z > SOFTPLUS_LINEAR, z, jnp.log(1.0 + jnp.exp(z)))
                if diagonal:
                    sp = jnp.where(causal, sp, 0.0)
                z_buf[slot, h] = z
                sp_buf[slot, h] = sp.astype(BF16)

        def apply(m, slot, diagonal):
            k0 = pl.multiple_of(q0 - m * blk, blk)
            for h in range(ATTN_HEADS):
                vb = v_ref[0, pl.ds(k0, blk), tile_of(h)]
                c = jnp.dot(sp_buf[slot, h], tri, preferred_element_type=F32)
                carry = carry_ref[h]
                a = jnp.exp(z_buf[slot, h] - c - jnp.concatenate([carry] * (blk // LANES), axis=1))
                if diagonal:
                    a = jnp.where(causal, a, 0.0)
                acc_ref[h] += jnp.dot(a.astype(BF16), vb, preferred_element_type=F32)
                carry_ref[h] = carry + jnp.broadcast_to(c[:, 0:1], carry.shape)

        score(0, 0, True)

        @pl.when(n_back == 0)
        def _():
            apply(0, 0, True)

        @pl.when(n_back > 0)
        def _():
            score(1, 1, False)
            apply(0, 0, True)

            def pair(j, _):
                m = 1 + 2 * j
                score(m + 1, 0, False)
                apply(m, 1, False)
                score(m + 2, 1, False)
                apply(m + 1, 0, False)
                return 0

            lax.fori_loop(0, (n_back - 1) // 2, pair, 0)

            @pl.when(n_back % 2 == 0)
            def _():
                score(n_back, 0, False)
                apply(n_back - 1, 1, False)
                apply(n_back, 0, False)

            @pl.when(n_back % 2 == 1)
            def _():
                apply(n_back, 1, False)

        for t in range(ATTN_HEADS // per_tile):
            o = acc_ref[t * per_tile]
            for h in range(1, per_tile):
                o = jnp.where((lane // HEAD_DIM) == h, acc_ref[t * per_tile + h], o)
            o_ref[0, pl.ds(q0, blk), t * LANES:(t + 1) * LANES] = o.astype(BF16)
        return 0

    lax.fori_loop(0, s_len // blk, q_block, 0)


def _attention(qkv):
    b, s, _ = qkv.shape
    blk = min(ATTN_BLOCK, s)
    width = ATTN_HEADS * HEAD_DIM
    n_steps = D_ATTN // width
    return pl.pallas_call(
        functools.partial(_attn_kernel, blk=blk),
        grid=(b, n_steps),
        in_specs=[
            pl.BlockSpec((1, s, width), lambda i, p: (i, 0, p)),
            pl.BlockSpec((1, s, width), lambda i, p: (i, 0, n_steps + p)),
            pl.BlockSpec((1, s, width), lambda i, p: (i, 0, 2 * n_steps + p)),
        ],
        out_specs=pl.BlockSpec((1, s, width), lambda i, p: (i, 0, p)),
        out_shape=jax.ShapeDtypeStruct((b, s, D_ATTN), BF16),
        scratch_shapes=[pltpu.VMEM((ATTN_HEADS, blk, LANES), F32),
                        pltpu.VMEM((ATTN_HEADS, blk, LANES), F32),
                        pltpu.VMEM((2, ATTN_HEADS, blk, blk), F32),
                        pltpu.VMEM((2, ATTN_HEADS, blk, blk), BF16)],
        compiler_params=pltpu.CompilerParams(
            dimension_semantics=("arbitrary", "arbitrary"), vmem_limit_bytes=VMEM_LIMIT),
    )(qkv, qkv, qkv)


def _oproj_kernel(o_ref, x_ref, w_ref, lng_ref, lnb_ref, rw_ref, rb_ref, tri_ref,
                  h_ref, hp_ref, route_ref, route_t_ref, counts_ref, cnt_ref):
    mix = jnp.dot(o_ref[0], w_ref[...], preferred_element_type=F32)
    _finish(x_ref[0], mix, _is_first_step(), lng_ref, lnb_ref, rw_ref, rb_ref, tri_ref,
            h_ref, hp_ref, route_ref, route_t_ref, counts_ref, cnt_ref)


def _attn_out(o, x, w_o, ln_g, ln_b, router):
    b, s, d = x.shape
    ts = min(ROW_TILE, s)
    f_in, f_out, f_shape, f_scratch = _finish_specs(b, s, d, ts)
    return pl.pallas_call(
        _oproj_kernel,
        grid=(b, s // ts),
        in_specs=[
            pl.BlockSpec((1, ts, D_ATTN), lambda i, j: (i, j, 0)),
            pl.BlockSpec((1, ts, d), lambda i, j: (i, j, 0)),
            _const_spec((D_ATTN, d)),
            *f_in,
        ],
        out_specs=f_out,
        out_shape=f_shape,
        scratch_shapes=f_scratch,
        compiler_params=pltpu.CompilerParams(
            dimension_semantics=("arbitrary", "arbitrary"), vmem_limit_bytes=VMEM_LIMIT),
    )(o, x, w_o.astype(BF16), *_finish_args(ln_g, ln_b, router, ts))


def _dispatch_meta(route_t, counts, chunk):
    t = route_t.shape[1]
    sizes = counts[0, N_GROUPS:N_GROUPS + N_EXPERTS].astype(I32)
    ids = route_t[ROUTE_ID:ROUTE_ID + TOP_K].astype(I32)
    rank = route_t[ROUTE_RANK:ROUTE_RANK + TOP_K].astype(I32)
    padded = (sizes + chunk - 1) // chunk * chunk
    pad_end = jnp.cumsum(padded)
    pad_start = pad_end - padded
    experts = jnp.arange(N_EXPERTS, dtype=I32)
    seg_start = jnp.sum(jnp.where(ids[None] == experts[:, None, None], pad_start[:, None, None], 0), axis=0)
    dest = (seg_start + rank).astype(I32)
    n_chunks = -(-(t * TOP_K) // chunk) + N_EXPERTS
    chunk_first_row = jnp.arange(n_chunks, dtype=I32) * chunk
    chunk_expert = jnp.minimum(
        jnp.sum((chunk_first_row[:, None] >= pad_end[None, :]).astype(I32), axis=1), N_EXPERTS - 1)
    chunk_rows = jnp.clip(
        pad_start[chunk_expert] + sizes[chunk_expert] - chunk_first_row, 0, chunk).astype(I32)
    n_valid = (pad_end[-1] // chunk).astype(I32).reshape(1)
    return dest, chunk_expert, chunk_rows, n_valid, n_chunks


def _index_windows(idx):
    return jnp.pad(idx.reshape(-1, SC_WINDOW), ((0, 0), (0, LANES - SC_WINDOW)))


def _sc_row_scatter(x, dest, n_out):
    t, d = x.shape
    mesh = plsc.VectorSubcoreMesh(core_axis_name="core", subcore_axis_name="subcore")
    idx = [_index_windows(dest[k]) for k in range(TOP_K)]

    @pl.kernel(out_type=jax.ShapeDtypeStruct((n_out, d), x.dtype), mesh=mesh, scratch_types=[])
    def scatter(x_hbm, *rest):
        i_hbm, o_hbm = rest[:TOP_K], rest[TOP_K]

        def body(x_vmem, *i_vmem):
            for k in range(TOP_K):
                pltpu.sync_copy(x_vmem, o_hbm.at[i_vmem[k].at[0, pl.ds(0, SC_WINDOW)]])

        pltpu.emit_pipeline(
            body,
            grid=(t // SC_WINDOW,),
            in_specs=[pl.BlockSpec((SC_WINDOW, d), lambda i: (i, 0))]
            + [pl.BlockSpec((1, LANES), lambda i: (i, 0))] * TOP_K,
            out_specs=[],
            core_axis_name=("core", "subcore"),
            dimension_semantics=(pltpu.PARALLEL,),
        )(x_hbm, *i_hbm)

    return scatter(x, *idx)


def _sc_row_gather(x, idx):
    _, d = x.shape
    m = idx.shape[0]
    mesh = plsc.VectorSubcoreMesh(core_axis_name="core", subcore_axis_name="subcore")

    @pl.kernel(out_type=jax.ShapeDtypeStruct((m, d), x.dtype), mesh=mesh, scratch_types=[])
    def gather(x_hbm, i_hbm, o_hbm):
        def body(i_vmem, o_vmem):
            pltpu.sync_copy(x_hbm.at[i_vmem.at[0, pl.ds(0, SC_WINDOW)]], o_vmem)

        pltpu.emit_pipeline(
            body,
            grid=(m // SC_WINDOW,),
            in_specs=[pl.BlockSpec((1, LANES), lambda i: (i, 0))],
            out_specs=[pl.BlockSpec((SC_WINDOW, d), lambda i: (i, 0))],
            core_axis_name=("core", "subcore"),
            dimension_semantics=(pltpu.PARALLEL,),
        )(i_hbm, o_hbm)

    return gather(x, _index_windows(idx))


def _expert_kernel(ce_ref, rows_ref, nv_ref, x_ref, wup_ref, wdn_ref, y_ref, wup_bf, wdn_bf):
    i = pl.program_id(0)

    @pl.when(i < nv_ref[0])
    def _():
        prev = ce_ref[jnp.maximum(i - 1, 0)]

        @pl.when((i == 0) | (ce_ref[i] != prev))
        def _():
            wup_bf[...] = wup_ref[0, 0].astype(BF16)
            wdn_bf[...] = wdn_ref[0, 0].astype(BF16)

        row = lax.broadcasted_iota(I32, x_ref.shape, 0)
        packed = jnp.where(row < rows_ref[i], x_ref[...], jnp.uint32(0))
        gu = jnp.dot(_unpack_bf16_pairs(packed).astype(BF16), wup_bf[...], preferred_element_type=F32)
        gate, up = gu[:, :D_EXPERT], gu[:, D_EXPERT:]
        act = gate * jax.nn.sigmoid(gate) * up
        y = jnp.dot(act.astype(BF16), wdn_bf[...], preferred_element_type=F32)
        y_ref[...] = _pack_bf16_pairs(y)

    @pl.when(i >= nv_ref[0])
    def _():
        y_ref[...] = jnp.zeros_like(y_ref)


def _experts(x_rows, chunk_expert, chunk_rows, n_valid, n_chunks, chunk, layer, w_up, w_down):
    _, half = x_rows.shape
    d = 2 * half
    grid_spec = pltpu.PrefetchScalarGridSpec(
        num_scalar_prefetch=3,
        grid=(n_chunks,),
        in_specs=[
            pl.BlockSpec((chunk, half), lambda i, ce, rows, nv: (jnp.minimum(i, nv[0] - 1), 0)),
            pl.BlockSpec((1, 1, d, 2 * D_EXPERT), lambda i, ce, rows, nv: (layer, ce[i], 0, 0)),
            pl.BlockSpec((1, 1, D_EXPERT, d), lambda i, ce, rows, nv: (layer, ce[i], 0, 0)),
        ],
        out_specs=pl.BlockSpec((chunk, half), lambda i, ce, rows, nv: (i, 0)),
        scratch_shapes=[
            pltpu.VMEM((d, 2 * D_EXPERT), BF16),
            pltpu.VMEM((D_EXPERT, d), BF16),
        ],
    )
    return pl.pallas_call(
        _expert_kernel,
        grid_spec=grid_spec,
        out_shape=jax.ShapeDtypeStruct((n_chunks * chunk, half), U32),
        compiler_params=pltpu.CompilerParams(
            dimension_semantics=("arbitrary",), vmem_limit_bytes=VMEM_LIMIT),
    )(chunk_expert, chunk_rows, n_valid, x_rows, w_up, w_down)


def _combine_kernel(h_ref, route_ref, *rest):
    y_refs, (lng_ref, lnb_ref, o_ref) = rest[:TOP_K], rest[TOP_K:]
    gates = route_ref[:, ROUTE_GATE:ROUTE_GATE + TOP_K]
    ffn = gates[:, 0:1] * _unpack_bf16_pairs(y_refs[0][...])
    for k in range(1, TOP_K):
        ffn = ffn + gates[:, k:k + 1] * _unpack_bf16_pairs(y_refs[k][...])
    o_ref[...] = _layer_norm(DEEPNORM_ALPHA * h_ref[...] + ffn, lng_ref[...], lnb_ref[...])


def _combine(h2d, route, y_sel, ln_g, ln_b):
    t, d = h2d.shape
    tm = min(ROW_TILE, t)
    row = lambda v: v.reshape(1, -1)
    y_specs = [pl.BlockSpec((tm, d // 2), functools.partial(lambda i, k: (k * (t // tm) + i, 0), k=k))
               for k in range(TOP_K)]
    return pl.pallas_call(
        _combine_kernel,
        grid=(t // tm,),
        in_specs=[pl.BlockSpec((tm, d), lambda i: (i, 0)),
                  pl.BlockSpec((tm, ROUTER_COLS), lambda i: (i, 0)),
                  *y_specs, _const_spec((1, d)), _const_spec((1, d))],
        out_specs=pl.BlockSpec((tm, d), lambda i: (i, 0)),
        out_shape=jax.ShapeDtypeStruct((t, d), F32),
        compiler_params=pltpu.CompilerParams(
            dimension_semantics=("arbitrary",), vmem_limit_bytes=VMEM_LIMIT),
    )(h2d, route, *([y_sel] * TOP_K), row(ln_g), row(ln_b))


def _moe_block(h, h_packed, route, route_t, counts, layer, w_up, w_down, ln_g, ln_b):
    b, s, d = h.shape
    t = b * s
    chunk = min(EXPERT_CHUNK, t)
    dest, chunk_expert, chunk_rows, n_valid, n_chunks = _dispatch_meta(route_t, counts, chunk)
    x_rows = _sc_row_scatter(h_packed.reshape(t, d // 2), dest, n_chunks * chunk)
    y_rows = _experts(x_rows, chunk_expert, chunk_rows, n_valid, n_chunks, chunk, layer, w_up, w_down)
    y_sel = _sc_row_gather(y_rows, dest.reshape(-1))
    return _combine(h.reshape(t, d), route.reshape(t, ROUTER_COLS), y_sel, ln_g, ln_b).reshape(b, s, d)


def _router_params(rg_w, rg_b, re_w, re_b):
    w = jnp.concatenate([rg_w, re_w], axis=1)
    w = jnp.pad(w, ((0, 0), (0, ROUTER_COLS - w.shape[1])))
    hi = w.astype(BF16)
    lo = (w - hi.astype(F32)).astype(BF16)
    bias = jnp.concatenate([rg_b, re_b])
    bias = jnp.pad(bias, (0, ROUTER_COLS - bias.shape[0])).reshape(1, ROUTER_COLS)
    return jnp.concatenate([hi, lo], axis=1), bias


def kernel(x, even_w_in, even_conv_a, even_conv_b_w, even_conv_b_bias, even_norm_b_g,
           even_norm_b_b, even_w_out, odd_w_qkv, odd_w_o, ln_mix_g, ln_mix_b, ln_ffn_g,
           ln_ffn_b, router_group_w, router_group_b, router_expert_w, router_expert_b,
           expert_w_up, expert_w_down):
    for layer in range(DEPTH):
        i = layer // 2
        router = _router_params(router_group_w[layer], router_group_b[layer],
                                router_expert_w[layer], router_expert_b[layer])
        if layer % 2 == 0:
            h, hp, route, route_t, counts = _even_mixer(
                x, even_w_in[i], even_conv_a[i], even_conv_b_w[i], even_conv_b_bias[i],
                even_norm_b_g[i], even_norm_b_b[i], even_w_out[i], ln_mix_g[layer], ln_mix_b[layer],
                router)
        else:
            o = _attention(_qkv_proj(x, odd_w_qkv[i]))
            h, hp, route, route_t, counts = _attn_out(o, x, odd_w_o[i], ln_mix_g[layer], ln_mix_b[layer],
                                                      router)
        x = _moe_block(h, hp, route, route_t, counts, layer, expert_w_up, expert_w_down,
                       ln_ffn_g[layer], ln_ffn_b[layer])
    return x
```

```python
import functools

import jax
import jax.numpy as jnp
from jax import lax
from jax.experimental import pallas as pl
from jax.experimental.pallas import tpu as pltpu
from jax.experimental.pallas import tpu_sc as plsc

F32 = jnp.float32
BF16 = jnp.bfloat16
U32 = jnp.uint32
I32 = jnp.int32

D_MODEL = 1024
DEPTH = 4
D_CONV_A = 512
CONV_A_WIDTH = 3
D_CONV_B = 512
CONV_B_WIDTH = 31
N_HEADS = 16
HEAD_DIM = 64
D_ATTN = N_HEADS * HEAD_DIM
N_GROUPS = 4
EXPERTS_PER_GROUP = 8
N_EXPERTS = N_GROUPS * EXPERTS_PER_GROUP
TOP_K = 2
D_EXPERT = 512
LN_EPS = 1e-5
DEEPNORM_ALPHA = (2 * DEPTH) ** 0.25
D_IN_EVEN = 3 * D_CONV_A + 2 * D_CONV_B

LANES = 128
ROW_TILE = 512
CONV_HALO = 32
CONV_ROWS = 32
ATTN_BLOCK = 256
ATTN_HEADS = 4
EXPERT_CHUNK = 512
SC_WINDOW = 64
ROUTER_COLS = LANES
VMEM_LIMIT = 56 * 1024 * 1024
NEG_LARGE = -1e30
SOFTPLUS_LINEAR = 80.0
DECAY_STOP = 104.0

ROUTE_ID, ROUTE_GATE, ROUTE_RANK = 0, TOP_K, 2 * TOP_K
ROUTE_ROWS = 8


def _layer_norm(v, g, b):
    mu = jnp.mean(v, axis=-1, keepdims=True)
    c = v - mu
    var = jnp.mean(c * c, axis=-1, keepdims=True)
    return c * lax.rsqrt(var + LN_EPS) * g + b


def _pack_bf16_pairs(v):
    half = v.shape[1] // 2
    hi = lax.bitcast_convert_type(v[:, :half].astype(BF16).astype(F32), U32)
    lo = lax.bitcast_convert_type(v[:, half:].astype(BF16).astype(F32), U32)
    return hi | (lo >> 16)


def _unpack_bf16_pairs(w):
    hi = lax.bitcast_convert_type(w & jnp.uint32(0xFFFF0000), F32)
    lo = lax.bitcast_convert_type(w << 16, F32)
    return jnp.concatenate([hi, lo], axis=1)


def _router_logits(h, rw_ref):
    hi = h.astype(BF16)
    lo = (h - hi.astype(F32)).astype(BF16)
    r1 = jnp.dot(hi, rw_ref[...], preferred_element_type=F32)
    r2 = jnp.dot(lo, rw_ref[:, :ROUTER_COLS], preferred_element_type=F32)
    return r1[:, :ROUTER_COLS] + r1[:, ROUTER_COLS:] + r2


def _route(logits, tri_ref, base):
    lane = lax.broadcasted_iota(I32, logits.shape, 1)
    lane_f = lane.astype(F32)

    def top(vals):
        m = jnp.max(vals, axis=1, keepdims=True)
        return m, jnp.min(jnp.where(vals == m, lane_f, float(LANES)), axis=1, keepdims=True)

    is_group = lane < N_GROUPS
    g_max, grp = top(jnp.where(is_group, logits, NEG_LARGE))
    g_gate = 1.0 / jnp.sum(jnp.where(is_group, jnp.exp(logits - g_max), 0.0), axis=1, keepdims=True)
    first = N_GROUPS + EXPERTS_PER_GROUP * grp
    e_logits = jnp.where((lane_f >= first) & (lane_f < first + EXPERTS_PER_GROUP), logits, NEG_LARGE)
    v0, l0 = top(e_logits)
    v1, l1 = top(jnp.where(lane_f == l0, NEG_LARGE, e_logits))
    e = jnp.exp(v1 - v0)
    gate0 = g_gate / (1.0 + e)
    gate1 = gate0 * e

    hot0 = jnp.where(lane_f == l0, 1.0, 0.0)
    hot1 = jnp.where(lane_f == l1, 1.0, 0.0)
    before0 = jnp.dot(tri_ref[...], hot0.astype(BF16), preferred_element_type=F32)
    before1 = jnp.dot(tri_ref[...], hot1.astype(BF16), preferred_element_type=F32)
    n0 = jnp.sum(hot0, axis=0, keepdims=True)
    n1 = jnp.sum(hot1, axis=0, keepdims=True)
    rank0 = jnp.sum(hot0 * (before0 + base), axis=1, keepdims=True)
    rank1 = jnp.sum(hot1 * (before1 + base + n0), axis=1, keepdims=True)

    fields = [l0 - N_GROUPS, l1 - N_GROUPS, gate0, gate1, rank0, rank1]
    record = jnp.zeros(logits.shape, F32)
    for i, f in enumerate(fields):
        record = jnp.where(lane == i, f, record)
    return record, base + n0 + n1


def _finish(x, mix, is_first_step, lng_ref, lnb_ref, rw_ref, rb_ref, tri_ref,
            h_ref, hp_ref, route_ref, route_t_ref, counts_ref, cnt_ref):
    h = _layer_norm(DEEPNORM_ALPHA * x + mix, lng_ref[...], lnb_ref[...])
    h_ref[0] = h
    hp_ref[0] = _pack_bf16_pairs(h)

    @pl.when(is_first_step)
    def _():
        cnt_ref[...] = jnp.zeros_like(cnt_ref)

    record, counts = _route(_router_logits(h, rw_ref) + rb_ref[...], tri_ref, cnt_ref[...])
    route_ref[0] = record
    route_t_ref[...] = record.T[:ROUTE_ROWS]
    cnt_ref[...] = counts
    counts_ref[...] = counts


def _const_spec(shape):
    return pl.BlockSpec(shape, lambda *_: (0,) * len(shape))


def _finish_specs(b, s, d, ts):
    in_specs = [_const_spec((1, d)), _const_spec((1, d)), _const_spec((d, 2 * ROUTER_COLS)),
                _const_spec((1, ROUTER_COLS)), _const_spec((ts, ts))]
    tile = lambda w: pl.BlockSpec((1, ts, w), lambda i, j: (i, j, 0))
    out_specs = [tile(d), tile(d // 2), tile(ROUTER_COLS),
                 pl.BlockSpec((ROUTE_ROWS, ts), lambda i, j: (0, i * (s // ts) + j)),
                 _const_spec((1, ROUTER_COLS))]
    out_shape = [jax.ShapeDtypeStruct((b, s, d), F32), jax.ShapeDtypeStruct((b, s, d // 2), U32),
                 jax.ShapeDtypeStruct((b, s, ROUTER_COLS), F32),
                 jax.ShapeDtypeStruct((ROUTE_ROWS, b * s), F32),
                 jax.ShapeDtypeStruct((1, ROUTER_COLS), F32)]
    return in_specs, out_specs, out_shape, [pltpu.VMEM((1, ROUTER_COLS), F32)]


def _finish_args(ln_g, ln_b, router, ts):
    rw, rb = router
    tri = (lax.broadcasted_iota(I32, (ts, ts), 1) < lax.broadcasted_iota(I32, (ts, ts), 0)).astype(BF16)
    return ln_g.reshape(1, -1), ln_b.reshape(1, -1), rw, rb, tri


def _is_first_step():
    return (pl.program_id(0) == 0) & (pl.program_id(1) == 0)


def _even_kernel(x_ref, win_ref, ca_ref, cbw_ref, cbb_ref, nbg_ref, nbb_ref, wout_ref,
                 lng_ref, lnb_ref, rw_ref, rb_ref, tri_ref,
                 h_ref, hp_ref, route_ref, route_t_ref, counts_ref,
                 cnt_ref, u_buf, cv_buf, g_buf, y_buf):
    s = pl.program_id(1)
    ts = x_ref.shape[1]
    x = x_ref[0]
    u_buf[...] = jnp.dot(x.astype(BF16), win_ref[...], preferred_element_type=F32)

    @pl.when(s == 0)
    def _():
        cv_buf[0:CONV_HALO] = jnp.zeros((CONV_HALO, D_CONV_A), F32)
        g_buf[0:CONV_HALO] = jnp.zeros((CONV_HALO, D_CONV_B), F32)

    @pl.when(s > 0)
    def _():
        cv_buf[0:CONV_HALO] = cv_buf[ts:ts + CONV_HALO]
        g_buf[0:CONV_HALO] = g_buf[ts:ts + CONV_HALO]

    c0 = D_CONV_A
    cv_buf[CONV_HALO:CONV_HALO + ts] = u_buf[:, c0:2 * c0] * u_buf[:, 2 * c0:3 * c0]
    b_val = u_buf[:, 3 * c0:3 * c0 + D_CONV_B]
    b_gate = u_buf[:, 3 * c0 + D_CONV_B:]
    g_buf[CONV_HALO:CONV_HALO + ts] = b_val * jax.nn.sigmoid(b_gate)

    for r0 in range(0, ts, CONV_ROWS):
        acc_a = None
        for k in range(CONV_A_WIDTH):
            off = CONV_HALO + r0 - (CONV_A_WIDTH - 1) + k
            term = ca_ref[k:k + 1, :] * cv_buf[off:off + CONV_ROWS, :]
            acc_a = term if acc_a is None else acc_a + term
        y_a = u_buf[r0:r0 + CONV_ROWS, 0:c0] * acc_a
        acc_b = None
        for k in range(CONV_B_WIDTH):
            off = CONV_HALO + r0 - (CONV_B_WIDTH - 1) + k
            term = cbw_ref[k:k + 1, :] * g_buf[off:off + CONV_ROWS, :]
            acc_b = term if acc_b is None else acc_b + term
        gb = _layer_norm(acc_b + cbb_ref[...], nbg_ref[...], nbb_ref[...])
        y_b = gb * jax.nn.sigmoid(gb)
        y_buf[r0:r0 + CONV_ROWS, 0:c0] = y_a.astype(BF16)
        y_buf[r0:r0 + CONV_ROWS, c0:] = y_b.astype(BF16)

    mix = jnp.dot(y_buf[...], wout_ref[...], preferred_element_type=F32)
    _finish(x, mix, _is_first_step(), lng_ref, lnb_ref, rw_ref, rb_ref, tri_ref,
            h_ref, hp_ref, route_ref, route_t_ref, counts_ref, cnt_ref)


def _even_mixer(x, w_in, conv_a, conv_b_w, conv_b_bias, norm_b_g, norm_b_b, w_out, ln_g, ln_b, router):
    b, s, d = x.shape
    ts = min(ROW_TILE, s)
    row = lambda v: v.reshape(1, -1)
    f_in, f_out, f_shape, f_scratch = _finish_specs(b, s, d, ts)
    return pl.pallas_call(
        _even_kernel,
        grid=(b, s // ts),
        in_specs=[
            pl.BlockSpec((1, ts, d), lambda i, j: (i, j, 0)),
            _const_spec((d, D_IN_EVEN)),
            _const_spec((CONV_A_WIDTH, D_CONV_A)),
            _const_spec((CONV_B_WIDTH, D_CONV_B)),
            _const_spec((1, D_CONV_B)),
            _const_spec((1, D_CONV_B)),
            _const_spec((1, D_CONV_B)),
            _const_spec((D_CONV_A + D_CONV_B, d)),
            *f_in,
        ],
        out_specs=f_out,
        out_shape=f_shape,
        scratch_shapes=[
            *f_scratch,
            pltpu.VMEM((ts, D_IN_EVEN), F32),
            pltpu.VMEM((CONV_HALO + ts, D_CONV_A), F32),
            pltpu.VMEM((CONV_HALO + ts, D_CONV_B), F32),
            pltpu.VMEM((ts, D_CONV_A + D_CONV_B), BF16),
        ],
        compiler_params=pltpu.CompilerParams(
            dimension_semantics=("arbitrary", "arbitrary"), vmem_limit_bytes=VMEM_LIMIT),
    )(x, w_in.astype(BF16), conv_a, conv_b_w, row(conv_b_bias), row(norm_b_g), row(norm_b_b),
      w_out.astype(BF16), *_finish_args(ln_g, ln_b, router, ts))


def _qkv_kernel(x_ref, w_ref, o_ref):
    u = jnp.dot(x_ref[0].astype(BF16), w_ref[...], preferred_element_type=F32)
    o_ref[0, :, :D_ATTN] = (u[:, :D_ATTN] * (HEAD_DIM ** -0.5)).astype(BF16)
    o_ref[0, :, D_ATTN:] = u[:, D_ATTN:].astype(BF16)


def _qkv_proj(x, w_qkv):
    b, s, d = x.shape
    ts = min(ROW_TILE, s)
    return pl.pallas_call(
        _qkv_kernel,
        grid=(b, s // ts),
        in_specs=[pl.BlockSpec((1, ts, d), lambda i, j: (i, j, 0)), _const_spec((d, 3 * D_ATTN))],
        out_specs=pl.BlockSpec((1, ts, 3 * D_ATTN), lambda i, j: (i, j, 0)),
        out_shape=jax.ShapeDtypeStruct((b, s, 3 * D_ATTN), BF16),
        compiler_params=pltpu.CompilerParams(
            dimension_semantics=("arbitrary", "arbitrary"), vmem_limit_bytes=VMEM_LIMIT),
    )(x, w_qkv.astype(BF16))


def _attn_kernel(q_ref, k_ref, v_ref, o_ref, acc_ref, carry_ref, z_buf, sp_buf, *, blk):
    s_len = q_ref.shape[1]
    per_tile = LANES // HEAD_DIM
    lane = lax.broadcasted_iota(jnp.int32, (1, LANES), 1)
    tri = (lax.broadcasted_iota(jnp.int32, (blk, blk), 0)
           >= lax.broadcasted_iota(jnp.int32, (blk, blk), 1)).astype(BF16)
    causal = (lax.broadcasted_iota(jnp.int32, (blk, blk), 1)
              < lax.broadcasted_iota(jnp.int32, (blk, blk), 0))

    def tile_of(h):
        return slice((h // per_tile) * LANES, (h // per_tile + 1) * LANES)

    def q_block(n_back, _):
        q0 = pl.multiple_of(n_back * blk, blk)
        qms = [jnp.where((lane // HEAD_DIM) == h % per_tile, q_ref[0, pl.ds(q0, blk), tile_of(h)],
                         jnp.zeros((), BF16)) for h in range(ATTN_HEADS)]
        acc_ref[...] = jnp.zeros_like(acc_ref)
        carry_ref[...] = jnp.zeros_like(carry_ref)

        def score(m, slot, diagonal):
            k0 = pl.multiple_of(q0 - m * blk, blk)
            for h in range(ATTN_HEADS):
                kb = k_ref[0, pl.ds(k0, blk), tile_of(h)]
                z = lax.dot_general(qms[h], kb, (((1,), (1,)), ((), ())),
                                    preferred_element_type=F32)
                sp = jnp.where(z > SOFTPLUS_LINEAR, z, jnp.log(1.0 + jnp.exp(z)))
                if diagonal:
                    sp = jnp.where(causal, sp, 0.0)
                z_buf[slot, h] = z
                sp_buf[slot, h] = sp.astype(BF16)

        def apply(m, slot, diagonal):
            k0 = pl.multiple_of(q0 - m * blk, blk)
            for h in range(ATTN_HEADS):
                vb = v_ref[0, pl.ds(k0, blk), tile_of(h)]
                c = jnp.dot(sp_buf[slot, h], tri, preferred_element_type=F32)
                carry = carry_ref[h]
                a = jnp.exp(z_buf[slot, h] - c - jnp.concatenate([carry] * (blk // LANES), axis=1))
                if diagonal:
                    a = jnp.where(causal, a, 0.0)
                acc_ref[h] += jnp.dot(a.astype(BF16), vb, preferred_element_type=F32)
                carry_ref[h] = carry + jnp.broadcast_to(c[:, 0:1], carry.shape)

        score(0, 0, True)

        @pl.when(n_back == 0)
        def _():
            apply(0, 0, True)

        def live():
            return (jnp.min(carry_ref[...]) <= DECAY_STOP).astype(jnp.int32)

        @pl.when(n_back > 0)
        def _():
            score(1, 1, False)
            apply(0, 0, True)

            def pair(state):
                m, _ = state
                score(m + 1, 0, False)
                apply(m, 1, False)

                @pl.when((live() == 1) & (m + 2 <= n_back))
                def _():
                    score(m + 2, 1, False)
                    apply(m + 1, 0, False)

                return m + 2, live()

            _, alive = lax.while_loop(lambda st: (st[0] + 1 <= n_back) & (st[1] == 1), pair,
                                      (jnp.int32(1), jnp.int32(1)))

            @pl.when((alive == 1) & (n_back % 2 == 0))
            def _():
                apply(n_back, 0, False)

            @pl.when((alive == 1) & (n_back % 2 == 1))
            def _():
                apply(n_back, 1, False)

        for t in range(ATTN_HEADS // per_tile):
            o = acc_ref[t * per_tile]
            for h in range(1, per_tile):
                o = jnp.where((lane // HEAD_DIM) == h, acc_ref[t * per_tile + h], o)
            o_ref[0, pl.ds(q0, blk), t * LANES:(t + 1) * LANES] = o.astype(BF16)
        return 0

    lax.fori_loop(0, s_len // blk, q_block, 0)


def _attention(qkv):
    b, s, _ = qkv.shape
    blk = min(ATTN_BLOCK, s)
    width = ATTN_HEADS * HEAD_DIM
    n_steps = D_ATTN // width
    return pl.pallas_call(
        functools.partial(_attn_kernel, blk=blk),
        grid=(b, n_steps),
        in_specs=[
            pl.BlockSpec((1, s, width), lambda i, p: (i, 0, p)),
            pl.BlockSpec((1, s, width), lambda i, p: (i, 0, n_steps + p)),
            pl.BlockSpec((1, s, width), lambda i, p: (i, 0, 2 * n_steps + p)),
        ],
        out_specs=pl.BlockSpec((1, s, width), lambda i, p: (i, 0, p)),
        out_shape=jax.ShapeDtypeStruct((b, s, D_ATTN), BF16),
        scratch_shapes=[pltpu.VMEM((ATTN_HEADS, blk, LANES), F32),
                        pltpu.VMEM((ATTN_HEADS, blk, LANES), F32),
                        pltpu.VMEM((2, ATTN_HEADS, blk, blk), F32),
                        pltpu.VMEM((2, ATTN_HEADS, blk, blk), BF16)],
        compiler_params=pltpu.CompilerParams(
            dimension_semantics=("arbitrary", "arbitrary"), vmem_limit_bytes=VMEM_LIMIT),
    )(qkv, qkv, qkv)


def _oproj_kernel(o_ref, x_ref, w_ref, lng_ref, lnb_ref, rw_ref, rb_ref, tri_ref,
                  h_ref, hp_ref, route_ref, route_t_ref, counts_ref, cnt_ref):
    mix = jnp.dot(o_ref[0], w_ref[...], preferred_element_type=F32)
    _finish(x_ref[0], mix, _is_first_step(), lng_ref, lnb_ref, rw_ref, rb_ref, tri_ref,
            h_ref, hp_ref, route_ref, route_t_ref, counts_ref, cnt_ref)


def _attn_out(o, x, w_o, ln_g, ln_b, router):
    b, s, d = x.shape
    ts = min(ROW_TILE, s)
    f_in, f_out, f_shape, f_scratch = _finish_specs(b, s, d, ts)
    return pl.pallas_call(
        _oproj_kernel,
        grid=(b, s // ts),
        in_specs=[
            pl.BlockSpec((1, ts, D_ATTN), lambda i, j: (i, j, 0)),
            pl.BlockSpec((1, ts, d), lambda i, j: (i, j, 0)),
            _const_spec((D_ATTN, d)),
            *f_in,
        ],
        out_specs=f_out,
        out_shape=f_shape,
        scratch_shapes=f_scratch,
        compiler_params=pltpu.CompilerParams(
            dimension_semantics=("arbitrary", "arbitrary"), vmem_limit_bytes=VMEM_LIMIT),
    )(o, x, w_o.astype(BF16), *_finish_args(ln_g, ln_b, router, ts))


def _dispatch_meta(route_t, counts, chunk):
    t = route_t.shape[1]
    sizes = counts[0, N_GROUPS:N_GROUPS + N_EXPERTS].astype(I32)
    ids = route_t[ROUTE_ID:ROUTE_ID + TOP_K].astype(I32)
    rank = route_t[ROUTE_RANK:ROUTE_RANK + TOP_K].astype(I32)
    padded = (sizes + chunk - 1) // chunk * chunk
    pad_end = jnp.cumsum(padded)
    pad_start = pad_end - padded
    experts = jnp.arange(N_EXPERTS, dtype=I32)
    seg_start = jnp.sum(jnp.where(ids[None] == experts[:, None, None], pad_start[:, None, None], 0), axis=0)
    dest = (seg_start + rank).astype(I32)
    n_chunks = -(-(t * TOP_K) // chunk) + N_EXPERTS
    chunk_first_row = jnp.arange(n_chunks, dtype=I32) * chunk
    chunk_expert = jnp.minimum(
        jnp.sum((chunk_first_row[:, None] >= pad_end[None, :]).astype(I32), axis=1), N_EXPERTS - 1)
    chunk_rows = jnp.clip(
        pad_start[chunk_expert] + sizes[chunk_expert] - chunk_first_row, 0, chunk).astype(I32)
    n_valid = (pad_end[-1] // chunk).astype(I32).reshape(1)
    return dest, chunk_expert, chunk_rows, n_valid, n_chunks


def _index_windows(idx):
    return jnp.pad(idx.reshape(-1, SC_WINDOW), ((0, 0), (0, LANES - SC_WINDOW)))


def _sc_row_scatter(x, dest, n_out):
    t, d = x.shape
    mesh = plsc.VectorSubcoreMesh(core_axis_name="core", subcore_axis_name="subcore")
    idx = [_index_windows(dest[k]) for k in range(TOP_K)]

    @pl.kernel(out_type=jax.ShapeDtypeStruct((n_out, d), x.dtype), mesh=mesh, scratch_types=[])
    def scatter(x_hbm, *rest):
        i_hbm, o_hbm = rest[:TOP_K], rest[TOP_K]

        def body(x_vmem, *i_vmem):
            for k in range(TOP_K):
                pltpu.sync_copy(x_vmem, o_hbm.at[i_vmem[k].at[0, pl.ds(0, SC_WINDOW)]])

        pltpu.emit_pipeline(
            body,
            grid=(t // SC_WINDOW,),
            in_specs=[pl.BlockSpec((SC_WINDOW, d), lambda i: (i, 0))]
            + [pl.BlockSpec((1, LANES), lambda i: (i, 0))] * TOP_K,
            out_specs=[],
            core_axis_name=("core", "subcore"),
            dimension_semantics=(pltpu.PARALLEL,),
        )(x_hbm, *i_hbm)

    return scatter(x, *idx)


def _sc_row_gather(x, idx):
    _, d = x.shape
    m = idx.shape[0]
    mesh = plsc.VectorSubcoreMesh(core_axis_name="core", subcore_axis_name="subcore")

    @pl.kernel(out_type=jax.ShapeDtypeStruct((m, d), x.dtype), mesh=mesh, scratch_types=[])
    def gather(x_hbm, i_hbm, o_hbm):
        def body(i_vmem, o_vmem):
            pltpu.sync_copy(x_hbm.at[i_vmem.at[0, pl.ds(0, SC_WINDOW)]], o_vmem)

        pltpu.emit_pipeline(
            body,
            grid=(m // SC_WINDOW,),
            in_specs=[pl.BlockSpec((1, LANES), lambda i: (i, 0))],
            out_specs=[pl.BlockSpec((SC_WINDOW, d), lambda i: (i, 0))],
            core_axis_name=("core", "subcore"),
            dimension_semantics=(pltpu.PARALLEL,),
        )(i_hbm, o_hbm)

    return gather(x, _index_windows(idx))


def _expert_kernel(ce_ref, rows_ref, nv_ref, x_ref, wup_ref, wdn_ref, y_ref, wup_bf, wdn_bf):
    i = pl.program_id(0)

    @pl.when(i < nv_ref[0])
    def _():
        prev = ce_ref[jnp.maximum(i - 1, 0)]

        @pl.when((i == 0) | (ce_ref[i] != prev))
        def _():
            wup_bf[...] = wup_ref[0, 0].astype(BF16)
            wdn_bf[...] = wdn_ref[0, 0].astype(BF16)

        row = lax.broadcasted_iota(I32, x_ref.shape, 0)
        packed = jnp.where(row < rows_ref[i], x_ref[...], jnp.uint32(0))
        gu = jnp.dot(_unpack_bf16_pairs(packed).astype(BF16), wup_bf[...], preferred_element_type=F32)
        gate, up = gu[:, :D_EXPERT], gu[:, D_EXPERT:]
        act = gate * jax.nn.sigmoid(gate) * up
        y = jnp.dot(act.astype(BF16), wdn_bf[...], preferred_element_type=F32)
        y_ref[...] = _pack_bf16_pairs(y)

    @pl.when(i >= nv_ref[0])
    def _():
        y_ref[...] = jnp.zeros_like(y_ref)


def _experts(x_rows, chunk_expert, chunk_rows, n_valid, n_chunks, chunk, layer, w_up, w_down):
    _, half = x_rows.shape
    d = 2 * half
    grid_spec = pltpu.PrefetchScalarGridSpec(
        num_scalar_prefetch=3,
        grid=(n_chunks,),
        in_specs=[
            pl.BlockSpec((chunk, half), lambda i, ce, rows, nv: (jnp.minimum(i, nv[0] - 1), 0)),
            pl.BlockSpec((1, 1, d, 2 * D_EXPERT), lambda i, ce, rows, nv: (layer, ce[i], 0, 0)),
            pl.BlockSpec((1, 1, D_EXPERT, d), lambda i, ce, rows, nv: (layer, ce[i], 0, 0)),
        ],
        out_specs=pl.BlockSpec((chunk, half), lambda i, ce, rows, nv: (i, 0)),
        scratch_shapes=[
            pltpu.VMEM((d, 2 * D_EXPERT), BF16),
            pltpu.VMEM((D_EXPERT, d), BF16),
        ],
    )
    return pl.pallas_call(
        _expert_kernel,
        grid_spec=grid_spec,
        out_shape=jax.ShapeDtypeStruct((n_chunks * chunk, half), U32),
        compiler_params=pltpu.CompilerParams(
            dimension_semantics=("arbitrary",), vmem_limit_bytes=VMEM_LIMIT),
    )(chunk_expert, chunk_rows, n_valid, x_rows, w_up, w_down)


def _combine_kernel(h_ref, route_ref, *rest):
    y_refs, (lng_ref, lnb_ref, o_ref) = rest[:TOP_K], rest[TOP_K:]
    gates = route_ref[:, ROUTE_GATE:ROUTE_GATE + TOP_K]
    ffn = gates[:, 0:1] * _unpack_bf16_pairs(y_refs[0][...])
    for k in range(1, TOP_K):
        ffn = ffn + gates[:, k:k + 1] * _unpack_bf16_pairs(y_refs[k][...])
    o_ref[...] = _layer_norm(DEEPNORM_ALPHA * h_ref[...] + ffn, lng_ref[...], lnb_ref[...])


def _combine(h2d, route, y_sel, ln_g, ln_b):
    t, d = h2d.shape
    tm = min(ROW_TILE, t)
    row = lambda v: v.reshape(1, -1)
    y_specs = [pl.BlockSpec((tm, d // 2), functools.partial(lambda i, k: (k * (t // tm) + i, 0), k=k))
               for k in range(TOP_K)]
    return pl.pallas_call(
        _combine_kernel,
        grid=(t // tm,),
        in_specs=[pl.BlockSpec((tm, d), lambda i: (i, 0)),
                  pl.BlockSpec((tm, ROUTER_COLS), lambda i: (i, 0)),
                  *y_specs, _const_spec((1, d)), _const_spec((1, d))],
        out_specs=pl.BlockSpec((tm, d), lambda i: (i, 0)),
        out_shape=jax.ShapeDtypeStruct((t, d), F32),
        compiler_params=pltpu.CompilerParams(
            dimension_semantics=("arbitrary",), vmem_limit_bytes=VMEM_LIMIT),
    )(h2d, route, *([y_sel] * TOP_K), row(ln_g), row(ln_b))


def _moe_block(h, h_packed, route, route_t, counts, layer, w_up, w_down, ln_g, ln_b):
    b, s, d = h.shape
    t = b * s
    chunk = min(EXPERT_CHUNK, t)
    dest, chunk_expert, chunk_rows, n_valid, n_chunks = _dispatch_meta(route_t, counts, chunk)
    x_rows = _sc_row_scatter(h_packed.reshape(t, d // 2), dest, n_chunks * chunk)
    y_rows = _experts(x_rows, chunk_expert, chunk_rows, n_valid, n_chunks, chunk, layer, w_up, w_down)
    y_sel = _sc_row_gather(y_rows, dest.reshape(-1))
    return _combine(h.reshape(t, d), route.reshape(t, ROUTER_COLS), y_sel, ln_g, ln_b).reshape(b, s, d)


def _router_params(rg_w, rg_b, re_w, re_b):
    w = jnp.concatenate([rg_w, re_w], axis=1)
    w = jnp.pad(w, ((0, 0), (0, ROUTER_COLS - w.shape[1])))
    hi = w.astype(BF16)
    lo = (w - hi.astype(F32)).astype(BF16)
    bias = jnp.concatenate([rg_b, re_b])
    bias = jnp.pad(bias, (0, ROUTER_COLS - bias.shape[0])).reshape(1, ROUTER_COLS)
    return jnp.concatenate([hi, lo], axis=1), bias


def kernel(x, even_w_in, even_conv_a, even_conv_b_w, even_conv_b_bias, even_norm_b_g,
           even_norm_b_b, even_w_out, odd_w_qkv, odd_w_o, ln_mix_g, ln_mix_b, ln_ffn_g,
           ln_ffn_b, router_group_w, router_group_b, router_expert_w, router_expert_b,
           expert_w_up, expert_w_down):
    for layer in range(DEPTH):
        i = layer // 2
        router = _router_params(router_group_w[layer], router_group_b[layer],
                                router_expert_w[layer], router_expert_b[layer])
        if layer % 2 == 0:
            h, hp, route, route_t, counts = _even_mixer(
                x, even_w_in[i], even_conv_a[i], even_conv_b_w[i], even_conv_b_bias[i],
                even_norm_b_g[i], even_norm_b_b[i], even_w_out[i], ln_mix_g[layer], ln_mix_b[layer],
                router)
        else:
            o = _attention(_qkv_proj(x, odd_w_qkv[i]))
            h, hp, route, route_t, counts = _attn_out(o, x, odd_w_o[i], ln_mix_g[layer], ln_mix_b[layer],
                                                      router)
        x = _moe_block(h, hp, route, route_t, counts, layer, expert_w_up, expert_w_down,
                       ln_ffn_g[layer], ln_ffn_b[layer])
    return x
```

```python
import functools

import jax
import jax.numpy as jnp
from jax import lax
from jax.experimental import pallas as pl
from jax.experimental.pallas import tpu as pltpu
from jax.experimental.pallas import tpu_sc as plsc

F32 = jnp.float32
BF16 = jnp.bfloat16
U32 = jnp.uint32
I32 = jnp.int32

D_MODEL = 1024
DEPTH = 4
D_CONV_A = 512
CONV_A_WIDTH = 3
D_CONV_B = 512
CONV_B_WIDTH = 31
N_HEADS = 16
HEAD_DIM = 64
D_ATTN = N_HEADS * HEAD_DIM
N_GROUPS = 4
EXPERTS_PER_GROUP = 8
N_EXPERTS = N_GROUPS * EXPERTS_PER_GROUP
TOP_K = 2
D_EXPERT = 512
LN_EPS = 1e-5
DEEPNORM_ALPHA = (2 * DEPTH) ** 0.25
D_IN_EVEN = 3 * D_CONV_A + 2 * D_CONV_B

LANES = 128
SUBLANES = 8
ROW_TILE = 512
CONV_HALO = 32
CONV_ROWS = 32
EVEN_SECTION = 128
ATTN_BLOCK = 256
ATTN_HEADS = 8
EXPERT_CHUNK = 512
SC_WINDOW = 64
ROUTER_COLS = LANES
VMEM_LIMIT = 56 * 1024 * 1024
NEG_LARGE = -1e30
SOFTPLUS_LINEAR = 80.0
DECAY_STOP = 104.0

ROUTE_ID, ROUTE_GATE, ROUTE_RANK = 0, TOP_K, 2 * TOP_K
ROUTE_ROWS = 8


def _layer_norm(v, g, b):
    mu = jnp.mean(v, axis=-1, keepdims=True)
    c = v - mu
    var = jnp.mean(c * c, axis=-1, keepdims=True)
    return c * lax.rsqrt(var + LN_EPS) * g + b


def _pack_bf16_pairs(v):
    half = v.shape[1] // 2
    hi = lax.bitcast_convert_type(v[:, :half].astype(BF16).astype(F32), U32)
    lo = lax.bitcast_convert_type(v[:, half:].astype(BF16).astype(F32), U32)
    return hi | (lo >> 16)


def _unpack_bf16_pairs(w):
    hi = lax.bitcast_convert_type(w & jnp.uint32(0xFFFF0000), F32)
    lo = lax.bitcast_convert_type(w << 16, F32)
    return jnp.concatenate([hi, lo], axis=1)


def _router_logits(h, rw_ref):
    hi = h.astype(BF16)
    lo = (h - hi.astype(F32)).astype(BF16)
    r1 = jnp.dot(hi, rw_ref[...], preferred_element_type=F32)
    r2 = jnp.dot(lo, rw_ref[:, :ROUTER_COLS], preferred_element_type=F32)
    return r1[:, :ROUTER_COLS] + r1[:, ROUTER_COLS:] + r2


def _route(logits, tri_ref, base):
    lane = lax.broadcasted_iota(I32, logits.shape, 1)
    lane_f = lane.astype(F32)

    def top(vals):
        m = jnp.max(vals, axis=1, keepdims=True)
        return m, jnp.min(jnp.where(vals == m, lane_f, float(LANES)), axis=1, keepdims=True)

    is_group = lane < N_GROUPS
    g_max, grp = top(jnp.where(is_group, logits, NEG_LARGE))
    g_gate = 1.0 / jnp.sum(jnp.where(is_group, jnp.exp(logits - g_max), 0.0), axis=1, keepdims=True)
    first = N_GROUPS + EXPERTS_PER_GROUP * grp
    e_logits = jnp.where((lane_f >= first) & (lane_f < first + EXPERTS_PER_GROUP), logits, NEG_LARGE)
    v0, l0 = top(e_logits)
    v1, l1 = top(jnp.where(lane_f == l0, NEG_LARGE, e_logits))
    e = jnp.exp(v1 - v0)
    gate0 = g_gate / (1.0 + e)
    gate1 = gate0 * e

    hot0 = jnp.where(lane_f == l0, 1.0, 0.0)
    hot1 = jnp.where(lane_f == l1, 1.0, 0.0)
    before0 = jnp.dot(tri_ref[...], hot0.astype(BF16), preferred_element_type=F32)
    before1 = jnp.dot(tri_ref[...], hot1.astype(BF16), preferred_element_type=F32)
    n0 = jnp.sum(hot0, axis=0, keepdims=True)
    n1 = jnp.sum(hot1, axis=0, keepdims=True)
    rank0 = jnp.sum(hot0 * (before0 + base), axis=1, keepdims=True)
    rank1 = jnp.sum(hot1 * (before1 + base + n0), axis=1, keepdims=True)

    fields = [l0 - N_GROUPS, l1 - N_GROUPS, gate0, gate1, rank0, rank1]
    record = jnp.zeros(logits.shape, F32)
    for i, f in enumerate(fields):
        record = jnp.where(lane == i, f, record)
    return record, base + n0 + n1


def _finish(x, mix, is_first_step, lng_ref, lnb_ref, rw_ref, rb_ref, tri_ref,
            h_ref, hp_ref, route_ref, route_t_ref, counts_ref, cnt_ref):
    h = _layer_norm(DEEPNORM_ALPHA * x + mix, lng_ref[...], lnb_ref[...])
    h_ref[0] = h
    hp_ref[0] = _pack_bf16_pairs(h)

    @pl.when(is_first_step)
    def _():
        cnt_ref[...] = jnp.zeros_like(cnt_ref)

    record, counts = _route(_router_logits(h, rw_ref) + rb_ref[...], tri_ref, cnt_ref[...])
    route_ref[0] = record
    route_t_ref[...] = record.T[:ROUTE_ROWS]
    cnt_ref[...] = counts
    counts_ref[...] = counts


def _const_spec(shape):
    return pl.BlockSpec(shape, lambda *_: (0,) * len(shape))


def _finish_specs(b, s, d, ts):
    in_specs = [_const_spec((1, d)), _const_spec((1, d)), _const_spec((d, 2 * ROUTER_COLS)),
                _const_spec((1, ROUTER_COLS)), _const_spec((ts, ts))]
    tile = lambda w: pl.BlockSpec((1, ts, w), lambda i, j: (i, j, 0))
    out_specs = [tile(d), tile(d // 2), tile(ROUTER_COLS),
                 pl.BlockSpec((ROUTE_ROWS, ts), lambda i, j: (0, i * (s // ts) + j)),
                 _const_spec((1, ROUTER_COLS))]
    out_shape = [jax.ShapeDtypeStruct((b, s, d), F32), jax.ShapeDtypeStruct((b, s, d // 2), U32),
                 jax.ShapeDtypeStruct((b, s, ROUTER_COLS), F32),
                 jax.ShapeDtypeStruct((ROUTE_ROWS, b * s), F32),
                 jax.ShapeDtypeStruct((1, ROUTER_COLS), F32)]
    return in_specs, out_specs, out_shape, [pltpu.VMEM((1, ROUTER_COLS), F32)]


def _finish_args(ln_g, ln_b, router, ts):
    rw, rb = router
    tri = (lax.broadcasted_iota(I32, (ts, ts), 1) < lax.broadcasted_iota(I32, (ts, ts), 0)).astype(BF16)
    return ln_g.reshape(1, -1), ln_b.reshape(1, -1), rw, rb, tri


def _is_first_step():
    return (pl.program_id(0) == 0) & (pl.program_id(1) == 0)


class _ShiftedWindow:
    def __init__(self, buf, start, back):
        self.back = back
        self.window = buf[start - back:start + CONV_ROWS, :]
        self.rolled = {0: self.window}

    def rows(self, d):
        first = self.back + d
        shift = first % SUBLANES
        if shift not in self.rolled:
            n, c = self.window.shape
            tiles = pltpu.roll(self.window.reshape(n // SUBLANES, SUBLANES, c), SUBLANES - shift, axis=1)
            sub = lax.broadcasted_iota(I32, (1, SUBLANES, 1), 1)
            self.rolled[shift] = jnp.where(sub < SUBLANES - shift, tiles[:-1], tiles[1:]).reshape(n - SUBLANES, c)
        return self.rolled[shift][first - shift:first - shift + CONV_ROWS]


def _even_kernel(x_ref, win_ref, ca_ref, cbw_ref, cbb_ref, nbg_ref, nbb_ref, wout_ref,
                 lng_ref, lnb_ref, rw_ref, rb_ref, tri_ref,
                 h_ref, hp_ref, route_ref, route_t_ref, counts_ref,
                 cnt_ref, u_buf, cv_buf, g_buf, y_buf):
    s = pl.program_id(1)
    ts = x_ref.shape[1]
    x = x_ref[0]

    @pl.when(s == 0)
    def _():
        cv_buf[0:CONV_HALO] = jnp.zeros((CONV_HALO, D_CONV_A), F32)
        g_buf[0:CONV_HALO] = jnp.zeros((CONV_HALO, D_CONV_B), F32)

    @pl.when(s > 0)
    def _():
        cv_buf[0:CONV_HALO] = cv_buf[ts:ts + CONV_HALO]
        g_buf[0:CONV_HALO] = g_buf[ts:ts + CONV_HALO]

    c0 = D_CONV_A
    sec = min(EVEN_SECTION, ts)
    for r0 in range(0, ts, CONV_ROWS):
        if r0 % sec == 0:
            rows = slice(r0, r0 + sec)
            halo_rows = slice(CONV_HALO + r0, CONV_HALO + r0 + sec)
            u_buf[rows] = jnp.dot(x_ref[0, rows, :].astype(BF16), win_ref[...], preferred_element_type=F32)
            cv_buf[halo_rows] = u_buf[rows, c0:2 * c0] * u_buf[rows, 2 * c0:3 * c0]
            b_val = u_buf[rows, 3 * c0:3 * c0 + D_CONV_B]
            b_gate = u_buf[rows, 3 * c0 + D_CONV_B:]
            g_buf[halo_rows] = b_val * jax.nn.sigmoid(b_gate)
        win_a = _ShiftedWindow(cv_buf, CONV_HALO + r0, SUBLANES)
        acc_a = None
        for k in range(CONV_A_WIDTH):
            term = ca_ref[k:k + 1, :] * win_a.rows(k - (CONV_A_WIDTH - 1))
            acc_a = term if acc_a is None else acc_a + term
        y_a = u_buf[r0:r0 + CONV_ROWS, 0:c0] * acc_a
        win_b = _ShiftedWindow(g_buf, CONV_HALO + r0, CONV_HALO)
        acc_b = None
        for k in range(CONV_B_WIDTH):
            term = cbw_ref[k:k + 1, :] * win_b.rows(k - (CONV_B_WIDTH - 1))
            acc_b = term if acc_b is None else acc_b + term
        gb = _layer_norm(acc_b + cbb_ref[...], nbg_ref[...], nbb_ref[...])
        y_b = gb * jax.nn.sigmoid(gb)
        y_buf[r0:r0 + CONV_ROWS, 0:c0] = y_a.astype(BF16)
        y_buf[r0:r0 + CONV_ROWS, c0:] = y_b.astype(BF16)

    mix = jnp.dot(y_buf[...], wout_ref[...], preferred_element_type=F32)
    _finish(x, mix, _is_first_step(), lng_ref, lnb_ref, rw_ref, rb_ref, tri_ref,
            h_ref, hp_ref, route_ref, route_t_ref, counts_ref, cnt_ref)


def _even_mixer(x, w_in, conv_a, conv_b_w, conv_b_bias, norm_b_g, norm_b_b, w_out, ln_g, ln_b, router):
    b, s, d = x.shape
    ts = min(ROW_TILE, s)
    row = lambda v: v.reshape(1, -1)
    f_in, f_out, f_shape, f_scratch = _finish_specs(b, s, d, ts)
    return pl.pallas_call(
        _even_kernel,
        grid=(b, s // ts),
        in_specs=[
            pl.BlockSpec((1, ts, d), lambda i, j: (i, j, 0)),
            _const_spec((d, D_IN_EVEN)),
            _const_spec((CONV_A_WIDTH, D_CONV_A)),
            _const_spec((CONV_B_WIDTH, D_CONV_B)),
            _const_spec((1, D_CONV_B)),
            _const_spec((1, D_CONV_B)),
            _const_spec((1, D_CONV_B)),
            _const_spec((D_CONV_A + D_CONV_B, d)),
            *f_in,
        ],
        out_specs=f_out,
        out_shape=f_shape,
        scratch_shapes=[
            *f_scratch,
            pltpu.VMEM((ts, D_IN_EVEN), F32),
            pltpu.VMEM((CONV_HALO + ts, D_CONV_A), F32),
            pltpu.VMEM((CONV_HALO + ts, D_CONV_B), F32),
            pltpu.VMEM((ts, D_CONV_A + D_CONV_B), BF16),
        ],
        compiler_params=pltpu.CompilerParams(
            dimension_semantics=("arbitrary", "arbitrary"), vmem_limit_bytes=VMEM_LIMIT),
    )(x, w_in.astype(BF16), conv_a, conv_b_w, row(conv_b_bias), row(norm_b_g), row(norm_b_b),
      w_out.astype(BF16), *_finish_args(ln_g, ln_b, router, ts))


def _qkv_kernel(x_ref, w_ref, o_ref):
    u = jnp.dot(x_ref[0].astype(BF16), w_ref[...], preferred_element_type=F32)
    o_ref[0, :, :D_ATTN] = (u[:, :D_ATTN] * (HEAD_DIM ** -0.5)).astype(BF16)
    o_ref[0, :, D_ATTN:] = u[:, D_ATTN:].astype(BF16)


def _qkv_proj(x, w_qkv):
    b, s, d = x.shape
    ts = min(ROW_TILE, s)
    return pl.pallas_call(
        _qkv_kernel,
        grid=(b, s // ts),
        in_specs=[pl.BlockSpec((1, ts, d), lambda i, j: (i, j, 0)), _const_spec((d, 3 * D_ATTN))],
        out_specs=pl.BlockSpec((1, ts, 3 * D_ATTN), lambda i, j: (i, j, 0)),
        out_shape=jax.ShapeDtypeStruct((b, s, 3 * D_ATTN), BF16),
        compiler_params=pltpu.CompilerParams(
            dimension_semantics=("arbitrary", "arbitrary"), vmem_limit_bytes=VMEM_LIMIT),
    )(x, w_qkv.astype(BF16))


def _attn_kernel(q_ref, k_ref, v_ref, o_ref, acc_ref, carry_ref, z_buf, sp_buf, *, blk):
    s_len = q_ref.shape[1]
    per_tile = LANES // HEAD_DIM
    lane = lax.broadcasted_iota(jnp.int32, (1, LANES), 1)
    tri = (lax.broadcasted_iota(jnp.int32, (blk, blk), 0)
           >= lax.broadcasted_iota(jnp.int32, (blk, blk), 1)).astype(BF16)
    causal = (lax.broadcasted_iota(jnp.int32, (blk, blk), 1)
              < lax.broadcasted_iota(jnp.int32, (blk, blk), 0))

    def tile_of(h):
        return slice((h // per_tile) * LANES, (h // per_tile + 1) * LANES)

    def q_block(n_back, _):
        q0 = pl.multiple_of(n_back * blk, blk)
        qms = [jnp.where((lane // HEAD_DIM) == h % per_tile, q_ref[0, pl.ds(q0, blk), tile_of(h)],
                         jnp.zeros((), BF16)) for h in range(ATTN_HEADS)]
        acc_ref[...] = jnp.zeros_like(acc_ref)
        carry_ref[...] = jnp.zeros_like(carry_ref)

        def score(m, slot, diagonal):
            k0 = pl.multiple_of(q0 - m * blk, blk)
            for h in range(ATTN_HEADS):
                kb = k_ref[0, pl.ds(k0, blk), tile_of(h)]
                z = lax.dot_general(qms[h], kb, (((1,), (1,)), ((), ())),
                                    preferred_element_type=F32)
                sp = jnp.where(z > SOFTPLUS_LINEAR, z, jnp.log(1.0 + jnp.exp(z)))
                if diagonal:
                    sp = jnp.where(causal, sp, 0.0)
                z_buf[slot, h] = z
                sp_buf[slot, h] = sp.astype(BF16)

        def apply(m, slot, diagonal):
            k0 = pl.multiple_of(q0 - m * blk, blk)
            for h in range(ATTN_HEADS):
                vb = v_ref[0, pl.ds(k0, blk), tile_of(h)]
                c = jnp.dot(sp_buf[slot, h], tri, preferred_element_type=F32)
                carry = carry_ref[h]
                a = jnp.exp(z_buf[slot, h] - c - jnp.concatenate([carry] * (blk // LANES), axis=1))
                if diagonal:
                    a = jnp.where(causal, a, 0.0)
                acc_ref[h] += jnp.dot(a.astype(BF16), vb, preferred_element_type=F32)
                carry_ref[h] = carry + jnp.broadcast_to(c[:, 0:1], carry.shape)

        score(0, 0, True)

        @pl.when(n_back == 0)
        def _():
            apply(0, 0, True)

        def live():
            return (jnp.min(carry_ref[...]) <= DECAY_STOP).astype(jnp.int32)

        @pl.when(n_back > 0)
        def _():
            score(1, 1, False)
            apply(0, 0, True)

            def pair(state):
                m, _ = state
                score(m + 1, 0, False)
                apply(m, 1, False)

                @pl.when((live() == 1) & (m + 2 <= n_back))
                def _():
                    score(m + 2, 1, False)
                    apply(m + 1, 0, False)

                return m + 2, live()

            _, alive = lax.while_loop(lambda st: (st[0] + 1 <= n_back) & (st[1] == 1), pair,
                                      (jnp.int32(1), jnp.int32(1)))

            @pl.when((alive == 1) & (n_back % 2 == 0))
            def _():
                apply(n_back, 0, False)

            @pl.when((alive == 1) & (n_back % 2 == 1))
            def _():
                apply(n_back, 1, False)

        for t in range(ATTN_HEADS // per_tile):
            o = acc_ref[t * per_tile]
            for h in range(1, per_tile):
                o = jnp.where((lane // HEAD_DIM) == h, acc_ref[t * per_tile + h], o)
            o_ref[0, pl.ds(q0, blk), t * LANES:(t + 1) * LANES] = o.astype(BF16)
        return 0

    lax.fori_loop(0, s_len // blk, q_block, 0)


def _attention(qkv):
    b, s, _ = qkv.shape
    blk = min(ATTN_BLOCK, s)
    width = ATTN_HEADS * HEAD_DIM
    n_steps = D_ATTN // width
    return pl.pallas_call(
        functools.partial(_attn_kernel, blk=blk),
        grid=(b, n_steps),
        in_specs=[
            pl.BlockSpec((1, s, width), lambda i, p: (i, 0, p)),
            pl.BlockSpec((1, s, width), lambda i, p: (i, 0, n_steps + p)),
            pl.BlockSpec((1, s, width), lambda i, p: (i, 0, 2 * n_steps + p)),
        ],
        out_specs=pl.BlockSpec((1, s, width), lambda i, p: (i, 0, p)),
        out_shape=jax.ShapeDtypeStruct((b, s, D_ATTN), BF16),
        scratch_shapes=[pltpu.VMEM((ATTN_HEADS, blk, LANES), F32),
                        pltpu.VMEM((ATTN_HEADS, blk, LANES), F32),
                        pltpu.VMEM((2, ATTN_HEADS, blk, blk), F32),
                        pltpu.VMEM((2, ATTN_HEADS, blk, blk), BF16)],
        compiler_params=pltpu.CompilerParams(
            dimension_semantics=("arbitrary", "arbitrary"), vmem_limit_bytes=VMEM_LIMIT),
    )(qkv, qkv, qkv)


def _oproj_kernel(o_ref, x_ref, w_ref, lng_ref, lnb_ref, rw_ref, rb_ref, tri_ref,
                  h_ref, hp_ref, route_ref, route_t_ref, counts_ref, cnt_ref):
    mix = jnp.dot(o_ref[0], w_ref[...], preferred_element_type=F32)
    _finish(x_ref[0], mix, _is_first_step(), lng_ref, lnb_ref, rw_ref, rb_ref, tri_ref,
            h_ref, hp_ref, route_ref, route_t_ref, counts_ref, cnt_ref)


def _attn_out(o, x, w_o, ln_g, ln_b, router):
    b, s, d = x.shape
    ts = min(ROW_TILE, s)
    f_in, f_out, f_shape, f_scratch = _finish_specs(b, s, d, ts)
    return pl.pallas_call(
        _oproj_kernel,
        grid=(b, s // ts),
        in_specs=[
            pl.BlockSpec((1, ts, D_ATTN), lambda i, j: (i, j, 0)),
            pl.BlockSpec((1, ts, d), lambda i, j: (i, j, 0)),
            _const_spec((D_ATTN, d)),
            *f_in,
        ],
        out_specs=f_out,
        out_shape=f_shape,
        scratch_shapes=f_scratch,
        compiler_params=pltpu.CompilerParams(
            dimension_semantics=("arbitrary", "arbitrary"), vmem_limit_bytes=VMEM_LIMIT),
    )(o, x, w_o.astype(BF16), *_finish_args(ln_g, ln_b, router, ts))


def _dispatch_meta(route_t, counts, chunk):
    t = route_t.shape[1]
    sizes = counts[0, N_GROUPS:N_GROUPS + N_EXPERTS].astype(I32)
    ids = route_t[ROUTE_ID:ROUTE_ID + TOP_K].astype(I32)
    rank = route_t[ROUTE_RANK:ROUTE_RANK + TOP_K].astype(I32)
    padded = (sizes + chunk - 1) // chunk * chunk
    pad_end = jnp.cumsum(padded)
    pad_start = pad_end - padded
    experts = jnp.arange(N_EXPERTS, dtype=I32)
    seg_start = jnp.sum(jnp.where(ids[None] == experts[:, None, None], pad_start[:, None, None], 0), axis=0)
    dest = (seg_start + rank).astype(I32)
    n_chunks = -(-(t * TOP_K) // chunk) + N_EXPERTS
    chunk_first_row = jnp.arange(n_chunks, dtype=I32) * chunk
    chunk_expert = jnp.minimum(
        jnp.sum((chunk_first_row[:, None] >= pad_end[None, :]).astype(I32), axis=1), N_EXPERTS - 1)
    chunk_rows = jnp.clip(
        pad_start[chunk_expert] + sizes[chunk_expert] - chunk_first_row, 0, chunk).astype(I32)
    n_valid = (pad_end[-1] // chunk).astype(I32).reshape(1)
    return dest, chunk_expert, chunk_rows, n_valid, n_chunks


def _index_windows(idx):
    return jnp.pad(idx.reshape(-1, SC_WINDOW), ((0, 0), (0, LANES - SC_WINDOW)))


def _sc_row_scatter(x, dest, n_out):
    t, d = x.shape
    mesh = plsc.VectorSubcoreMesh(core_axis_name="core", subcore_axis_name="subcore")
    idx = [_index_windows(dest[k]) for k in range(TOP_K)]

    @pl.kernel(out_type=jax.ShapeDtypeStruct((n_out, d), x.dtype), mesh=mesh, scratch_types=[])
    def scatter(x_hbm, *rest):
        i_hbm, o_hbm = rest[:TOP_K], rest[TOP_K]

        def body(x_vmem, *i_vmem):
            for k in range(TOP_K):
                pltpu.sync_copy(x_vmem, o_hbm.at[i_vmem[k].at[0, pl.ds(0, SC_WINDOW)]])

        pltpu.emit_pipeline(
            body,
            grid=(t // SC_WINDOW,),
            in_specs=[pl.BlockSpec((SC_WINDOW, d), lambda i: (i, 0))]
            + [pl.BlockSpec((1, LANES), lambda i: (i, 0))] * TOP_K,
            out_specs=[],
            core_axis_name=("core", "subcore"),
            dimension_semantics=(pltpu.PARALLEL,),
        )(x_hbm, *i_hbm)

    return scatter(x, *idx)


def _sc_row_gather(x, idx):
    _, d = x.shape
    m = idx.shape[0]
    mesh = plsc.VectorSubcoreMesh(core_axis_name="core", subcore_axis_name="subcore")

    @pl.kernel(out_type=jax.ShapeDtypeStruct((m, d), x.dtype), mesh=mesh, scratch_types=[])
    def gather(x_hbm, i_hbm, o_hbm):
        def body(i_vmem, o_vmem):
            pltpu.sync_copy(x_hbm.at[i_vmem.at[0, pl.ds(0, SC_WINDOW)]], o_vmem)

        pltpu.emit_pipeline(
            body,
            grid=(m // SC_WINDOW,),
            in_specs=[pl.BlockSpec((1, LANES), lambda i: (i, 0))],
            out_specs=[pl.BlockSpec((SC_WINDOW, d), lambda i: (i, 0))],
            core_axis_name=("core", "subcore"),
            dimension_semantics=(pltpu.PARALLEL,),
        )(i_hbm, o_hbm)

    return gather(x, _index_windows(idx))


def _expert_kernel(ce_ref, rows_ref, nv_ref, x_ref, wup_ref, wdn_ref, y_ref, wup_bf, wdn_bf):
    i = pl.program_id(0)

    @pl.when(i < nv_ref[0])
    def _():
        prev = ce_ref[jnp.maximum(i - 1, 0)]

        @pl.when((i == 0) | (ce_ref[i] != prev))
        def _():
            wup_bf[...] = wup_ref[0, 0].astype(BF16)
            wdn_bf[...] = wdn_ref[0, 0].astype(BF16)

        row = lax.broadcasted_iota(I32, x_ref.shape, 0)
        packed = jnp.where(row < rows_ref[i], x_ref[...], jnp.uint32(0))
        gu = jnp.dot(_unpack_bf16_pairs(packed).astype(BF16), wup_bf[...], preferred_element_type=F32)
        gate, up = gu[:, :D_EXPERT], gu[:, D_EXPERT:]
        act = gate * jax.nn.sigmoid(gate) * up
        y = jnp.dot(act.astype(BF16), wdn_bf[...], preferred_element_type=F32)
        y_ref[...] = _pack_bf16_pairs(y)

    @pl.when(i >= nv_ref[0])
    def _():
        y_ref[...] = jnp.zeros_like(y_ref)


def _experts(x_rows, chunk_expert, chunk_rows, n_valid, n_chunks, chunk, layer, w_up, w_down):
    _, half = x_rows.shape
    d = 2 * half
    grid_spec = pltpu.PrefetchScalarGridSpec(
        num_scalar_prefetch=3,
        grid=(n_chunks,),
        in_specs=[
            pl.BlockSpec((chunk, half), lambda i, ce, rows, nv: (jnp.minimum(i, nv[0] - 1), 0)),
            pl.BlockSpec((1, 1, d, 2 * D_EXPERT), lambda i, ce, rows, nv: (layer, ce[i], 0, 0)),
            pl.BlockSpec((1, 1, D_EXPERT, d), lambda i, ce, rows, nv: (layer, ce[i], 0, 0)),
        ],
        out_specs=pl.BlockSpec((chunk, half), lambda i, ce, rows, nv: (i, 0)),
        scratch_shapes=[
            pltpu.VMEM((d, 2 * D_EXPERT), BF16),
            pltpu.VMEM((D_EXPERT, d), BF16),
        ],
    )
    return pl.pallas_call(
        _expert_kernel,
        grid_spec=grid_spec,
        out_shape=jax.ShapeDtypeStruct((n_chunks * chunk, half), U32),
        compiler_params=pltpu.CompilerParams(
            dimension_semantics=("arbitrary",), vmem_limit_bytes=VMEM_LIMIT),
    )(chunk_expert, chunk_rows, n_valid, x_rows, w_up, w_down)


def _combine_kernel(h_ref, route_ref, *rest):
    y_refs, (lng_ref, lnb_ref, o_ref) = rest[:TOP_K], rest[TOP_K:]
    gates = route_ref[:, ROUTE_GATE:ROUTE_GATE + TOP_K]
    ffn = gates[:, 0:1] * _unpack_bf16_pairs(y_refs[0][...])
    for k in range(1, TOP_K):
        ffn = ffn + gates[:, k:k + 1] * _unpack_bf16_pairs(y_refs[k][...])
    o_ref[...] = _layer_norm(DEEPNORM_ALPHA * h_ref[...] + ffn, lng_ref[...], lnb_ref[...])


def _combine(h2d, route, y_sel, ln_g, ln_b):
    t, d = h2d.shape
    tm = min(ROW_TILE, t)
    row = lambda v: v.reshape(1, -1)
    y_specs = [pl.BlockSpec((tm, d // 2), functools.partial(lambda i, k: (k * (t // tm) + i, 0), k=k))
               for k in range(TOP_K)]
    return pl.pallas_call(
        _combine_kernel,
        grid=(t // tm,),
        in_specs=[pl.BlockSpec((tm, d), lambda i: (i, 0)),
                  pl.BlockSpec((tm, ROUTER_COLS), lambda i: (i, 0)),
                  *y_specs, _const_spec((1, d)), _const_spec((1, d))],
        out_specs=pl.BlockSpec((tm, d), lambda i: (i, 0)),
        out_shape=jax.ShapeDtypeStruct((t, d), F32),
        compiler_params=pltpu.CompilerParams(
            dimension_semantics=("arbitrary",), vmem_limit_bytes=VMEM_LIMIT),
    )(h2d, route, *([y_sel] * TOP_K), row(ln_g), row(ln_b))


def _moe_block(h, h_packed, route, route_t, counts, layer, w_up, w_down, ln_g, ln_b):
    b, s, d = h.shape
    t = b * s
    chunk = min(EXPERT_CHUNK, t)
    dest, chunk_expert, chunk_rows, n_valid, n_chunks = _dispatch_meta(route_t, counts, chunk)
    x_rows = _sc_row_scatter(h_packed.reshape(t, d // 2), dest, n_chunks * chunk)
    y_rows = _experts(x_rows, chunk_expert, chunk_rows, n_valid, n_chunks, chunk, layer, w_up, w_down)
    y_sel = _sc_row_gather(y_rows, dest.reshape(-1))
    return _combine(h.reshape(t, d), route.reshape(t, ROUTER_COLS), y_sel, ln_g, ln_b).reshape(b, s, d)


def _router_params(rg_w, rg_b, re_w, re_b):
    w = jnp.concatenate([rg_w, re_w], axis=1)
    w = jnp.pad(w, ((0, 0), (0, ROUTER_COLS - w.shape[1])))
    hi = w.astype(BF16)
    lo = (w - hi.astype(F32)).astype(BF16)
    bias = jnp.concatenate([rg_b, re_b])
    bias = jnp.pad(bias, (0, ROUTER_COLS - bias.shape[0])).reshape(1, ROUTER_COLS)
    return jnp.concatenate([hi, lo], axis=1), bias


def kernel(x, even_w_in, even_conv_a, even_conv_b_w, even_conv_b_bias, even_norm_b_g,
           even_norm_b_b, even_w_out, odd_w_qkv, odd_w_o, ln_mix_g, ln_mix_b, ln_ffn_g,
           ln_ffn_b, router_group_w, router_group_b, router_expert_w, router_expert_b,
           expert_w_up, expert_w_down):
    for layer in range(DEPTH):
        i = layer // 2
        router = _router_params(router_group_w[layer], router_group_b[layer],
                                router_expert_w[layer], router_expert_b[layer])
        if layer % 2 == 0:
            h, hp, route, route_t, counts = _even_mixer(
                x, even_w_in[i], even_conv_a[i], even_conv_b_w[i], even_conv_b_bias[i],
                even_norm_b_g[i], even_norm_b_b[i], even_w_out[i], ln_mix_g[layer], ln_mix_b[layer],
                router)
        else:
            o = _attention(_qkv_proj(x, odd_w_qkv[i]))
            h, hp, route, route_t, counts = _attn_out(o, x, odd_w_o[i], ln_mix_g[layer], ln_mix_b[layer],
                                                      router)
        x = _moe_block(h, hp, route, route_t, counts, layer, expert_w_up, expert_w_down,
                       ln_ffn_g[layer], ln_ffn_b[layer])
    return x
```

```python
import functools

import jax
import jax.numpy as jnp
from jax import lax
from jax.experimental import pallas as pl
from jax.experimental.pallas import tpu as pltpu
from jax.experimental.pallas import tpu_sc as plsc

F32 = jnp.float32
BF16 = jnp.bfloat16
U32 = jnp.uint32
I32 = jnp.int32

D_MODEL = 1024
DEPTH = 4
D_CONV_A = 512
CONV_A_WIDTH = 3
D_CONV_B = 512
CONV_B_WIDTH = 31
N_HEADS = 16
HEAD_DIM = 64
D_ATTN = N_HEADS * HEAD_DIM
N_GROUPS = 4
EXPERTS_PER_GROUP = 8
N_EXPERTS = N_GROUPS * EXPERTS_PER_GROUP
TOP_K = 2
D_EXPERT = 512
LN_EPS = 1e-5
DEEPNORM_ALPHA = (2 * DEPTH) ** 0.25
D_IN_EVEN = 3 * D_CONV_A + 2 * D_CONV_B

LANES = 128
SUBLANES = 8
ROW_TILE = 512
CONV_HALO = 32
CONV_ROWS = 32
EVEN_SECTION = 128
ATTN_BLOCK = 256
ATTN_HEADS = 8
EXPERT_CHUNK = 512
SC_WINDOW = 64
BATCH_STREAMS = 2
ROUTER_COLS = LANES
VMEM_LIMIT = 56 * 1024 * 1024
NEG_LARGE = -1e30
SOFTPLUS_LINEAR = 80.0
DECAY_STOP = 104.0

ROUTE_ID, ROUTE_GATE, ROUTE_RANK = 0, TOP_K, 2 * TOP_K
ROUTE_ROWS = 8


def _layer_norm(v, g, b):
    mu = jnp.mean(v, axis=-1, keepdims=True)
    c = v - mu
    var = jnp.mean(c * c, axis=-1, keepdims=True)
    return c * lax.rsqrt(var + LN_EPS) * g + b


def _pack_bf16_pairs(v):
    half = v.shape[1] // 2
    hi = lax.bitcast_convert_type(v[:, :half].astype(BF16).astype(F32), U32)
    lo = lax.bitcast_convert_type(v[:, half:].astype(BF16).astype(F32), U32)
    return hi | (lo >> 16)


def _unpack_bf16_pairs(w):
    hi = lax.bitcast_convert_type(w & jnp.uint32(0xFFFF0000), F32)
    lo = lax.bitcast_convert_type(w << 16, F32)
    return jnp.concatenate([hi, lo], axis=1)


def _router_logits(h, rw_ref):
    hi = h.astype(BF16)
    lo = (h - hi.astype(F32)).astype(BF16)
    r1 = jnp.dot(hi, rw_ref[...], preferred_element_type=F32)
    r2 = jnp.dot(lo, rw_ref[:, :ROUTER_COLS], preferred_element_type=F32)
    return r1[:, :ROUTER_COLS] + r1[:, ROUTER_COLS:] + r2


def _route(logits, tri_ref, base):
    lane = lax.broadcasted_iota(I32, logits.shape, 1)
    lane_f = lane.astype(F32)

    def top(vals):
        m = jnp.max(vals, axis=1, keepdims=True)
        return m, jnp.min(jnp.where(vals == m, lane_f, float(LANES)), axis=1, keepdims=True)

    is_group = lane < N_GROUPS
    g_max, grp = top(jnp.where(is_group, logits, NEG_LARGE))
    g_gate = 1.0 / jnp.sum(jnp.where(is_group, jnp.exp(logits - g_max), 0.0), axis=1, keepdims=True)
    first = N_GROUPS + EXPERTS_PER_GROUP * grp
    e_logits = jnp.where((lane_f >= first) & (lane_f < first + EXPERTS_PER_GROUP), logits, NEG_LARGE)
    v0, l0 = top(e_logits)
    v1, l1 = top(jnp.where(lane_f == l0, NEG_LARGE, e_logits))
    e = jnp.exp(v1 - v0)
    gate0 = g_gate / (1.0 + e)
    gate1 = gate0 * e

    hot0 = jnp.where(lane_f == l0, 1.0, 0.0)
    hot1 = jnp.where(lane_f == l1, 1.0, 0.0)
    before0 = jnp.dot(tri_ref[...], hot0.astype(BF16), preferred_element_type=F32)
    before1 = jnp.dot(tri_ref[...], hot1.astype(BF16), preferred_element_type=F32)
    n0 = jnp.sum(hot0, axis=0, keepdims=True)
    n1 = jnp.sum(hot1, axis=0, keepdims=True)
    rank0 = jnp.sum(hot0 * (before0 + base), axis=1, keepdims=True)
    rank1 = jnp.sum(hot1 * (before1 + base + n0), axis=1, keepdims=True)

    fields = [l0 - N_GROUPS, l1 - N_GROUPS, gate0, gate1, rank0, rank1]
    record = jnp.zeros(logits.shape, F32)
    for i, f in enumerate(fields):
        record = jnp.where(lane == i, f, record)
    return record, base + n0 + n1


def _finish(x, mix, is_first_step, lng_ref, lnb_ref, rw_ref, rb_ref, tri_ref,
            h_ref, hp_ref, route_ref, route_t_ref, counts_ref, cnt_ref):
    h = _layer_norm(DEEPNORM_ALPHA * x + mix, lng_ref[...], lnb_ref[...])
    h_ref[0] = h
    hp_ref[0] = _pack_bf16_pairs(h)

    @pl.when(is_first_step)
    def _():
        cnt_ref[...] = jnp.zeros_like(cnt_ref)

    record, counts = _route(_router_logits(h, rw_ref) + rb_ref[...], tri_ref, cnt_ref[...])
    route_ref[0] = record
    route_t_ref[...] = record.T[:ROUTE_ROWS]
    cnt_ref[...] = counts
    counts_ref[...] = counts


def _const_spec(shape):
    return pl.BlockSpec(shape, lambda *_: (0,) * len(shape))


def _finish_specs(b, s, d, ts):
    in_specs = [_const_spec((1, d)), _const_spec((1, d)), _const_spec((d, 2 * ROUTER_COLS)),
                _const_spec((1, ROUTER_COLS)), _const_spec((ts, ts))]
    tile = lambda w: pl.BlockSpec((1, ts, w), lambda i, j: (i, j, 0))
    out_specs = [tile(d), tile(d // 2), tile(ROUTER_COLS),
                 pl.BlockSpec((ROUTE_ROWS, ts), lambda i, j: (0, i * (s // ts) + j)),
                 _const_spec((1, ROUTER_COLS))]
    out_shape = [jax.ShapeDtypeStruct((b, s, d), F32), jax.ShapeDtypeStruct((b, s, d // 2), U32),
                 jax.ShapeDtypeStruct((b, s, ROUTER_COLS), F32),
                 jax.ShapeDtypeStruct((ROUTE_ROWS, b * s), F32),
                 jax.ShapeDtypeStruct((1, ROUTER_COLS), F32)]
    return in_specs, out_specs, out_shape, [pltpu.VMEM((1, ROUTER_COLS), F32)]


def _finish_args(ln_g, ln_b, router, ts):
    rw, rb = router
    tri = (lax.broadcasted_iota(I32, (ts, ts), 1) < lax.broadcasted_iota(I32, (ts, ts), 0)).astype(BF16)
    return ln_g.reshape(1, -1), ln_b.reshape(1, -1), rw, rb, tri


def _is_first_step():
    return (pl.program_id(0) == 0) & (pl.program_id(1) == 0)


class _ShiftedWindow:
    def __init__(self, buf, start, back):
        self.back = back
        self.window = buf[start - back:start + CONV_ROWS, :]
        self.rolled = {0: self.window}

    def rows(self, d):
        first = self.back + d
        shift = first % SUBLANES
        if shift not in self.rolled:
            n, c = self.window.shape
            tiles = pltpu.roll(self.window.reshape(n // SUBLANES, SUBLANES, c), SUBLANES - shift, axis=1)
            sub = lax.broadcasted_iota(I32, (1, SUBLANES, 1), 1)
            self.rolled[shift] = jnp.where(sub < SUBLANES - shift, tiles[:-1], tiles[1:]).reshape(n - SUBLANES, c)
        return self.rolled[shift][first - shift:first - shift + CONV_ROWS]


def _even_kernel(x_ref, win_ref, ca_ref, cbw_ref, cbb_ref, nbg_ref, nbb_ref, wout_ref,
                 lng_ref, lnb_ref, rw_ref, rb_ref, tri_ref,
                 h_ref, hp_ref, route_ref, route_t_ref, counts_ref,
                 cnt_ref, u_buf, cv_buf, g_buf, y_buf):
    s = pl.program_id(1)
    ts = x_ref.shape[1]
    x = x_ref[0]

    @pl.when(s == 0)
    def _():
        cv_buf[0:CONV_HALO] = jnp.zeros((CONV_HALO, D_CONV_A), F32)
        g_buf[0:CONV_HALO] = jnp.zeros((CONV_HALO, D_CONV_B), F32)

    @pl.when(s > 0)
    def _():
        cv_buf[0:CONV_HALO] = cv_buf[ts:ts + CONV_HALO]
        g_buf[0:CONV_HALO] = g_buf[ts:ts + CONV_HALO]

    c0 = D_CONV_A
    sec = min(EVEN_SECTION, ts)
    for r0 in range(0, ts, CONV_ROWS):
        if r0 % sec == 0:
            rows = slice(r0, r0 + sec)
            halo_rows = slice(CONV_HALO + r0, CONV_HALO + r0 + sec)
            u_buf[rows] = jnp.dot(x_ref[0, rows, :].astype(BF16), win_ref[...], preferred_element_type=F32)
            cv_buf[halo_rows] = u_buf[rows, c0:2 * c0] * u_buf[rows, 2 * c0:3 * c0]
            b_val = u_buf[rows, 3 * c0:3 * c0 + D_CONV_B]
            b_gate = u_buf[rows, 3 * c0 + D_CONV_B:]
            g_buf[halo_rows] = b_val * jax.nn.sigmoid(b_gate)
        win_a = _ShiftedWindow(cv_buf, CONV_HALO + r0, SUBLANES)
        acc_a = None
        for k in range(CONV_A_WIDTH):
            term = ca_ref[k:k + 1, :] * win_a.rows(k - (CONV_A_WIDTH - 1))
            acc_a = term if acc_a is None else acc_a + term
        y_a = u_buf[r0:r0 + CONV_ROWS, 0:c0] * acc_a
        win_b = _ShiftedWindow(g_buf, CONV_HALO + r0, CONV_HALO)
        acc_b = None
        for k in range(CONV_B_WIDTH):
            term = cbw_ref[k:k + 1, :] * win_b.rows(k - (CONV_B_WIDTH - 1))
            acc_b = term if acc_b is None else acc_b + term
        gb = _layer_norm(acc_b + cbb_ref[...], nbg_ref[...], nbb_ref[...])
        y_b = gb * jax.nn.sigmoid(gb)
        y_buf[r0:r0 + CONV_ROWS, 0:c0] = y_a.astype(BF16)
        y_buf[r0:r0 + CONV_ROWS, c0:] = y_b.astype(BF16)

    mix = jnp.dot(y_buf[...], wout_ref[...], preferred_element_type=F32)
    _finish(x, mix, _is_first_step(), lng_ref, lnb_ref, rw_ref, rb_ref, tri_ref,
            h_ref, hp_ref, route_ref, route_t_ref, counts_ref, cnt_ref)


def _even_mixer(x, w_in, conv_a, conv_b_w, conv_b_bias, norm_b_g, norm_b_b, w_out, ln_g, ln_b, router):
    b, s, d = x.shape
    ts = min(ROW_TILE, s)
    row = lambda v: v.reshape(1, -1)
    f_in, f_out, f_shape, f_scratch = _finish_specs(b, s, d, ts)
    return pl.pallas_call(
        _even_kernel,
        grid=(b, s // ts),
        in_specs=[
            pl.BlockSpec((1, ts, d), lambda i, j: (i, j, 0)),
            _const_spec((d, D_IN_EVEN)),
            _const_spec((CONV_A_WIDTH, D_CONV_A)),
            _const_spec((CONV_B_WIDTH, D_CONV_B)),
            _const_spec((1, D_CONV_B)),
            _const_spec((1, D_CONV_B)),
            _const_spec((1, D_CONV_B)),
            _const_spec((D_CONV_A + D_CONV_B, d)),
            *f_in,
        ],
        out_specs=f_out,
        out_shape=f_shape,
        scratch_shapes=[
            *f_scratch,
            pltpu.VMEM((ts, D_IN_EVEN), F32),
            pltpu.VMEM((CONV_HALO + ts, D_CONV_A), F32),
            pltpu.VMEM((CONV_HALO + ts, D_CONV_B), F32),
            pltpu.VMEM((ts, D_CONV_A + D_CONV_B), BF16),
        ],
        compiler_params=pltpu.CompilerParams(
            dimension_semantics=("arbitrary", "arbitrary"), vmem_limit_bytes=VMEM_LIMIT),
    )(x, w_in.astype(BF16), conv_a, conv_b_w, row(conv_b_bias), row(norm_b_g), row(norm_b_b),
      w_out.astype(BF16), *_finish_args(ln_g, ln_b, router, ts))


def _qkv_kernel(x_ref, w_ref, o_ref):
    u = jnp.dot(x_ref[0].astype(BF16), w_ref[...], preferred_element_type=F32)
    o_ref[0, :, :D_ATTN] = (u[:, :D_ATTN] * (HEAD_DIM ** -0.5)).astype(BF16)
    o_ref[0, :, D_ATTN:] = u[:, D_ATTN:].astype(BF16)


def _qkv_proj(x, w_qkv):
    b, s, d = x.shape
    ts = min(ROW_TILE, s)
    return pl.pallas_call(
        _qkv_kernel,
        grid=(b, s // ts),
        in_specs=[pl.BlockSpec((1, ts, d), lambda i, j: (i, j, 0)), _const_spec((d, 3 * D_ATTN))],
        out_specs=pl.BlockSpec((1, ts, 3 * D_ATTN), lambda i, j: (i, j, 0)),
        out_shape=jax.ShapeDtypeStruct((b, s, 3 * D_ATTN), BF16),
        compiler_params=pltpu.CompilerParams(
            dimension_semantics=("arbitrary", "arbitrary"), vmem_limit_bytes=VMEM_LIMIT),
    )(x, w_qkv.astype(BF16))


def _attn_kernel(q_ref, k_ref, v_ref, o_ref, acc_ref, carry_ref, z_buf, sp_buf, *, blk):
    s_len = q_ref.shape[1]
    per_tile = LANES // HEAD_DIM
    lane = lax.broadcasted_iota(jnp.int32, (1, LANES), 1)
    tri = (lax.broadcasted_iota(jnp.int32, (blk, blk), 0)
           >= lax.broadcasted_iota(jnp.int32, (blk, blk), 1)).astype(BF16)
    causal = (lax.broadcasted_iota(jnp.int32, (blk, blk), 1)
              < lax.broadcasted_iota(jnp.int32, (blk, blk), 0))

    def tile_of(h):
        return slice((h // per_tile) * LANES, (h // per_tile + 1) * LANES)

    def q_block(n_back, _):
        q0 = pl.multiple_of(n_back * blk, blk)
        qms = [jnp.where((lane // HEAD_DIM) == h % per_tile, q_ref[0, pl.ds(q0, blk), tile_of(h)],
                         jnp.zeros((), BF16)) for h in range(ATTN_HEADS)]
        acc_ref[...] = jnp.zeros_like(acc_ref)
        carry_ref[...] = jnp.zeros_like(carry_ref)

        def score(m, slot, diagonal):
            k0 = pl.multiple_of(q0 - m * blk, blk)
            for h in range(ATTN_HEADS):
                kb = k_ref[0, pl.ds(k0, blk), tile_of(h)]
                z = lax.dot_general(qms[h], kb, (((1,), (1,)), ((), ())),
                                    preferred_element_type=F32)
                sp = jnp.where(z > SOFTPLUS_LINEAR, z, jnp.log(1.0 + jnp.exp(z)))
                if diagonal:
                    sp = jnp.where(causal, sp, 0.0)
                z_buf[slot, h] = z
                sp_buf[slot, h] = sp.astype(BF16)

        def apply(m, slot, diagonal):
            k0 = pl.multiple_of(q0 - m * blk, blk)
            for h in range(ATTN_HEADS):
                vb = v_ref[0, pl.ds(k0, blk), tile_of(h)]
                c = jnp.dot(sp_buf[slot, h], tri, preferred_element_type=F32)
                carry = carry_ref[h]
                a = jnp.exp(z_buf[slot, h] - c - jnp.concatenate([carry] * (blk // LANES), axis=1))
                if diagonal:
                    a = jnp.where(causal, a, 0.0)
                acc_ref[h] += jnp.dot(a.astype(BF16), vb, preferred_element_type=F32)
                carry_ref[h] = carry + jnp.broadcast_to(c[:, 0:1], carry.shape)

        score(0, 0, True)

        @pl.when(n_back == 0)
        def _():
            apply(0, 0, True)

        def live():
            return (jnp.min(carry_ref[...]) <= DECAY_STOP).astype(jnp.int32)

        @pl.when(n_back > 0)
        def _():
            score(1, 1, False)
            apply(0, 0, True)
            apply(1, 1, False)

            @pl.when((n_back >= 2) & (live() == 1))
            def _():
                score(2, 0, False)

                def pair(state):
                    m, _ = state
                    score(m + 1, 1, False)
                    apply(m, 0, False)

                    @pl.when((live() == 1) & (m + 2 <= n_back))
                    def _():
                        score(m + 2, 0, False)
                        apply(m + 1, 1, False)

                    return m + 2, live()

                _, alive = lax.while_loop(lambda st: (st[0] + 1 <= n_back) & (st[1] == 1), pair,
                                          (jnp.int32(2), jnp.int32(1)))

                @pl.when((alive == 1) & (n_back % 2 == 0))
                def _():
                    apply(n_back, 0, False)

                @pl.when((alive == 1) & (n_back % 2 == 1))
                def _():
                    apply(n_back, 1, False)

        for t in range(ATTN_HEADS // per_tile):
            o = acc_ref[t * per_tile]
            for h in range(1, per_tile):
                o = jnp.where((lane // HEAD_DIM) == h, acc_ref[t * per_tile + h], o)
            o_ref[0, pl.ds(q0, blk), t * LANES:(t + 1) * LANES] = o.astype(BF16)
        return 0

    lax.fori_loop(0, s_len // blk, q_block, 0)


def _attention(qkv):
    b, s, _ = qkv.shape
    blk = min(ATTN_BLOCK, s)
    width = ATTN_HEADS * HEAD_DIM
    n_steps = D_ATTN // width
    return pl.pallas_call(
        functools.partial(_attn_kernel, blk=blk),
        grid=(b, n_steps),
        in_specs=[
            pl.BlockSpec((1, s, width), lambda i, p: (i, 0, p)),
            pl.BlockSpec((1, s, width), lambda i, p: (i, 0, n_steps + p)),
            pl.BlockSpec((1, s, width), lambda i, p: (i, 0, 2 * n_steps + p)),
        ],
        out_specs=pl.BlockSpec((1, s, width), lambda i, p: (i, 0, p)),
        out_shape=jax.ShapeDtypeStruct((b, s, D_ATTN), BF16),
        scratch_shapes=[pltpu.VMEM((ATTN_HEADS, blk, LANES), F32),
                        pltpu.VMEM((ATTN_HEADS, blk, LANES), F32),
                        pltpu.VMEM((2, ATTN_HEADS, blk, blk), F32),
                        pltpu.VMEM((2, ATTN_HEADS, blk, blk), BF16)],
        compiler_params=pltpu.CompilerParams(
            dimension_semantics=("arbitrary", "arbitrary"), vmem_limit_bytes=VMEM_LIMIT),
    )(qkv, qkv, qkv)


def _oproj_kernel(o_ref, x_ref, w_ref, lng_ref, lnb_ref, rw_ref, rb_ref, tri_ref,
                  h_ref, hp_ref, route_ref, route_t_ref, counts_ref, cnt_ref):
    mix = jnp.dot(o_ref[0], w_ref[...], preferred_element_type=F32)
    _finish(x_ref[0], mix, _is_first_step(), lng_ref, lnb_ref, rw_ref, rb_ref, tri_ref,
            h_ref, hp_ref, route_ref, route_t_ref, counts_ref, cnt_ref)


def _attn_out(o, x, w_o, ln_g, ln_b, router):
    b, s, d = x.shape
    ts = min(ROW_TILE, s)
    f_in, f_out, f_shape, f_scratch = _finish_specs(b, s, d, ts)
    return pl.pallas_call(
        _oproj_kernel,
        grid=(b, s // ts),
        in_specs=[
            pl.BlockSpec((1, ts, D_ATTN), lambda i, j: (i, j, 0)),
            pl.BlockSpec((1, ts, d), lambda i, j: (i, j, 0)),
            _const_spec((D_ATTN, d)),
            *f_in,
        ],
        out_specs=f_out,
        out_shape=f_shape,
        scratch_shapes=f_scratch,
        compiler_params=pltpu.CompilerParams(
            dimension_semantics=("arbitrary", "arbitrary"), vmem_limit_bytes=VMEM_LIMIT),
    )(o, x, w_o.astype(BF16), *_finish_args(ln_g, ln_b, router, ts))


def _dispatch_meta(route_t, counts, chunk):
    t = route_t.shape[1]
    sizes = counts[0, N_GROUPS:N_GROUPS + N_EXPERTS].astype(I32)
    ids = route_t[ROUTE_ID:ROUTE_ID + TOP_K].astype(I32)
    rank = route_t[ROUTE_RANK:ROUTE_RANK + TOP_K].astype(I32)
    padded = (sizes + chunk - 1) // chunk * chunk
    pad_end = jnp.cumsum(padded)
    pad_start = pad_end - padded
    experts = jnp.arange(N_EXPERTS, dtype=I32)
    seg_start = jnp.sum(jnp.where(ids[None] == experts[:, None, None], pad_start[:, None, None], 0), axis=0)
    dest = (seg_start + rank).astype(I32)
    n_chunks = -(-(t * TOP_K) // chunk) + N_EXPERTS
    chunk_first_row = jnp.arange(n_chunks, dtype=I32) * chunk
    chunk_expert = jnp.minimum(
        jnp.sum((chunk_first_row[:, None] >= pad_end[None, :]).astype(I32), axis=1), N_EXPERTS - 1)
    chunk_rows = jnp.clip(
        pad_start[chunk_expert] + sizes[chunk_expert] - chunk_first_row, 0, chunk).astype(I32)
    n_valid = (pad_end[-1] // chunk).astype(I32).reshape(1)
    return dest, chunk_expert, chunk_rows, n_valid, n_chunks


def _index_windows(idx):
    return jnp.pad(idx.reshape(-1, SC_WINDOW), ((0, 0), (0, LANES - SC_WINDOW)))


def _sc_row_scatter(x, dest, n_out):
    t, d = x.shape
    mesh = plsc.VectorSubcoreMesh(core_axis_name="core", subcore_axis_name="subcore")
    idx = [_index_windows(dest[k]) for k in range(TOP_K)]

    @pl.kernel(out_type=jax.ShapeDtypeStruct((n_out, d), x.dtype), mesh=mesh, scratch_types=[])
    def scatter(x_hbm, *rest):
        i_hbm, o_hbm = rest[:TOP_K], rest[TOP_K]

        def body(x_vmem, *i_vmem):
            for k in range(TOP_K):
                pltpu.sync_copy(x_vmem, o_hbm.at[i_vmem[k].at[0, pl.ds(0, SC_WINDOW)]])

        pltpu.emit_pipeline(
            body,
            grid=(t // SC_WINDOW,),
            in_specs=[pl.BlockSpec((SC_WINDOW, d), lambda i: (i, 0))]
            + [pl.BlockSpec((1, LANES), lambda i: (i, 0))] * TOP_K,
            out_specs=[],
            core_axis_name=("core", "subcore"),
            dimension_semantics=(pltpu.PARALLEL,),
        )(x_hbm, *i_hbm)

    return scatter(x, *idx)


def _sc_row_gather(x, idx):
    _, d = x.shape
    m = idx.shape[0]
    mesh = plsc.VectorSubcoreMesh(core_axis_name="core", subcore_axis_name="subcore")

    @pl.kernel(out_type=jax.ShapeDtypeStruct((m, d), x.dtype), mesh=mesh, scratch_types=[])
    def gather(x_hbm, i_hbm, o_hbm):
        def body(i_vmem, o_vmem):
            pltpu.sync_copy(x_hbm.at[i_vmem.at[0, pl.ds(0, SC_WINDOW)]], o_vmem)

        pltpu.emit_pipeline(
            body,
            grid=(m // SC_WINDOW,),
            in_specs=[pl.BlockSpec((1, LANES), lambda i: (i, 0))],
            out_specs=[pl.BlockSpec((SC_WINDOW, d), lambda i: (i, 0))],
            core_axis_name=("core", "subcore"),
            dimension_semantics=(pltpu.PARALLEL,),
        )(i_hbm, o_hbm)

    return gather(x, _index_windows(idx))


def _expert_kernel(ce_ref, rows_ref, nv_ref, x_ref, wup_ref, wdn_ref, y_ref, wup_bf, wdn_bf):
    i = pl.program_id(0)

    @pl.when(i < nv_ref[0])
    def _():
        prev = ce_ref[jnp.maximum(i - 1, 0)]

        @pl.when((i == 0) | (ce_ref[i] != prev))
        def _():
            wup_bf[...] = wup_ref[0, 0].astype(BF16)
            wdn_bf[...] = wdn_ref[0, 0].astype(BF16)

        row = lax.broadcasted_iota(I32, x_ref.shape, 0)
        packed = jnp.where(row < rows_ref[i], x_ref[...], jnp.uint32(0))
        gu = jnp.dot(_unpack_bf16_pairs(packed).astype(BF16), wup_bf[...], preferred_element_type=F32)
        gate, up = gu[:, :D_EXPERT], gu[:, D_EXPERT:]
        act = gate * jax.nn.sigmoid(gate) * up
        y = jnp.dot(act.astype(BF16), wdn_bf[...], preferred_element_type=F32)
        y_ref[...] = _pack_bf16_pairs(y)

    @pl.when(i >= nv_ref[0])
    def _():
        y_ref[...] = jnp.zeros_like(y_ref)


def _experts(x_rows, chunk_expert, chunk_rows, n_valid, n_chunks, chunk, layer, w_up, w_down):
    _, half = x_rows.shape
    d = 2 * half
    grid_spec = pltpu.PrefetchScalarGridSpec(
        num_scalar_prefetch=3,
        grid=(n_chunks,),
        in_specs=[
            pl.BlockSpec((chunk, half), lambda i, ce, rows, nv: (jnp.minimum(i, nv[0] - 1), 0)),
            pl.BlockSpec((1, 1, d, 2 * D_EXPERT), lambda i, ce, rows, nv: (layer, ce[i], 0, 0)),
            pl.BlockSpec((1, 1, D_EXPERT, d), lambda i, ce, rows, nv: (layer, ce[i], 0, 0)),
        ],
        out_specs=pl.BlockSpec((chunk, half), lambda i, ce, rows, nv: (i, 0)),
        scratch_shapes=[
            pltpu.VMEM((d, 2 * D_EXPERT), BF16),
            pltpu.VMEM((D_EXPERT, d), BF16),
        ],
    )
    return pl.pallas_call(
        _expert_kernel,
        grid_spec=grid_spec,
        out_shape=jax.ShapeDtypeStruct((n_chunks * chunk, half), U32),
        compiler_params=pltpu.CompilerParams(
            dimension_semantics=("arbitrary",), vmem_limit_bytes=VMEM_LIMIT),
    )(chunk_expert, chunk_rows, n_valid, x_rows, w_up, w_down)


def _combine_kernel(h_ref, route_ref, *rest):
    y_refs, (lng_ref, lnb_ref, o_ref) = rest[:TOP_K], rest[TOP_K:]
    gates = route_ref[:, ROUTE_GATE:ROUTE_GATE + TOP_K]
    ffn = gates[:, 0:1] * _unpack_bf16_pairs(y_refs[0][...])
    for k in range(1, TOP_K):
        ffn = ffn + gates[:, k:k + 1] * _unpack_bf16_pairs(y_refs[k][...])
    o_ref[...] = _layer_norm(DEEPNORM_ALPHA * h_ref[...] + ffn, lng_ref[...], lnb_ref[...])


def _combine(h2d, route, y_sel, ln_g, ln_b):
    t, d = h2d.shape
    tm = min(ROW_TILE, t)
    row = lambda v: v.reshape(1, -1)
    y_specs = [pl.BlockSpec((tm, d // 2), functools.partial(lambda i, k: (k * (t // tm) + i, 0), k=k))
               for k in range(TOP_K)]
    return pl.pallas_call(
        _combine_kernel,
        grid=(t // tm,),
        in_specs=[pl.BlockSpec((tm, d), lambda i: (i, 0)),
                  pl.BlockSpec((tm, ROUTER_COLS), lambda i: (i, 0)),
                  *y_specs, _const_spec((1, d)), _const_spec((1, d))],
        out_specs=pl.BlockSpec((tm, d), lambda i: (i, 0)),
        out_shape=jax.ShapeDtypeStruct((t, d), F32),
        compiler_params=pltpu.CompilerParams(
            dimension_semantics=("arbitrary",), vmem_limit_bytes=VMEM_LIMIT),
    )(h2d, route, *([y_sel] * TOP_K), row(ln_g), row(ln_b))


def _moe_block(h, h_packed, route, route_t, counts, layer, w_up, w_down, ln_g, ln_b):
    b, s, d = h.shape
    t = b * s
    chunk = min(EXPERT_CHUNK, t)
    dest, chunk_expert, chunk_rows, n_valid, n_chunks = _dispatch_meta(route_t, counts, chunk)
    x_rows = _sc_row_scatter(h_packed.reshape(t, d // 2), dest, n_chunks * chunk)
    y_rows = _experts(x_rows, chunk_expert, chunk_rows, n_valid, n_chunks, chunk, layer, w_up, w_down)
    y_sel = _sc_row_gather(y_rows, dest.reshape(-1))
    return _combine(h.reshape(t, d), route.reshape(t, ROUTER_COLS), y_sel, ln_g, ln_b).reshape(b, s, d)


def _router_params(rg_w, rg_b, re_w, re_b):
    w = jnp.concatenate([rg_w, re_w], axis=1)
    w = jnp.pad(w, ((0, 0), (0, ROUTER_COLS - w.shape[1])))
    hi = w.astype(BF16)
    lo = (w - hi.astype(F32)).astype(BF16)
    bias = jnp.concatenate([rg_b, re_b])
    bias = jnp.pad(bias, (0, ROUTER_COLS - bias.shape[0])).reshape(1, ROUTER_COLS)
    return jnp.concatenate([hi, lo], axis=1), bias


def kernel(x, even_w_in, even_conv_a, even_conv_b_w, even_conv_b_bias, even_norm_b_g,
           even_norm_b_b, even_w_out, odd_w_qkv, odd_w_o, ln_mix_g, ln_mix_b, ln_ffn_g,
           ln_ffn_b, router_group_w, router_group_b, router_expert_w, router_expert_b,
           expert_w_up, expert_w_down):
    n_streams = BATCH_STREAMS if x.shape[0] % BATCH_STREAMS == 0 else 1
    streams = jnp.split(x, n_streams, axis=0)
    for layer in range(DEPTH):
        i = layer // 2
        router = _router_params(router_group_w[layer], router_group_b[layer],
                                router_expert_w[layer], router_expert_b[layer])
        if layer % 2 == 0:
            mixed = [_even_mixer(
                xs, even_w_in[i], even_conv_a[i], even_conv_b_w[i], even_conv_b_bias[i],
                even_norm_b_g[i], even_norm_b_b[i], even_w_out[i], ln_mix_g[layer], ln_mix_b[layer],
                router) for xs in streams]
        else:
            mixed = [_attn_out(_attention(_qkv_proj(xs, odd_w_qkv[i])), xs, odd_w_o[i],
                               ln_mix_g[layer], ln_mix_b[layer], router) for xs in streams]
        streams = [_moe_block(*m, layer, expert_w_up, expert_w_down, ln_ffn_g[layer], ln_ffn_b[layer])
                   for m in mixed]
    return jnp.concatenate(streams, axis=0)
```

```python
import functools

import jax
import jax.numpy as jnp
from jax import lax
from jax.experimental import pallas as pl
from jax.experimental.pallas import tpu as pltpu
from jax.experimental.pallas import tpu_sc as plsc

F32 = jnp.float32
BF16 = jnp.bfloat16
U32 = jnp.uint32
I32 = jnp.int32

D_MODEL = 1024
DEPTH = 4
D_CONV_A = 512
CONV_A_WIDTH = 3
D_CONV_B = 512
CONV_B_WIDTH = 31
N_HEADS = 16
HEAD_DIM = 64
D_ATTN = N_HEADS * HEAD_DIM
N_GROUPS = 4
EXPERTS_PER_GROUP = 8
N_EXPERTS = N_GROUPS * EXPERTS_PER_GROUP
TOP_K = 2
D_EXPERT = 512
LN_EPS = 1e-5
DEEPNORM_ALPHA = (2 * DEPTH) ** 0.25
D_IN_EVEN = 3 * D_CONV_A + 2 * D_CONV_B

LANES = 128
SUBLANES = 8
ROW_TILE = 512
CONV_HALO = 32
CONV_ROWS = 32
EVEN_SECTION = 128
ATTN_BLOCK = 256
ATTN_HEADS = 8
EXPERT_CHUNK = 512
SC_WINDOW = 64
BATCH_STREAMS = 1
ROUTER_COLS = LANES
VMEM_LIMIT = 56 * 1024 * 1024
NEG_LARGE = -1e30
SOFTPLUS_LINEAR = 80.0
DECAY_STOP = 104.0

ROUTE_ID, ROUTE_GATE, ROUTE_RANK = 0, TOP_K, 2 * TOP_K
ROUTE_ROWS = 8


def _layer_norm(v, g, b):
    mu = jnp.mean(v, axis=-1, keepdims=True)
    c = v - mu
    var = jnp.mean(c * c, axis=-1, keepdims=True)
    return c * lax.rsqrt(var + LN_EPS) * g + b


def _pack_bf16_pairs(v):
    half = v.shape[1] // 2
    hi = lax.bitcast_convert_type(v[:, :half].astype(BF16).astype(F32), U32)
    lo = lax.bitcast_convert_type(v[:, half:].astype(BF16).astype(F32), U32)
    return hi | (lo >> 16)


def _unpack_bf16_pairs(w):
    hi = lax.bitcast_convert_type(w & jnp.uint32(0xFFFF0000), F32)
    lo = lax.bitcast_convert_type(w << 16, F32)
    return jnp.concatenate([hi, lo], axis=1)


def _router_logits(h, rw_ref):
    hi = h.astype(BF16)
    lo = (h - hi.astype(F32)).astype(BF16)
    r1 = jnp.dot(hi, rw_ref[...], preferred_element_type=F32)
    r2 = jnp.dot(lo, rw_ref[:, :ROUTER_COLS], preferred_element_type=F32)
    return r1[:, :ROUTER_COLS] + r1[:, ROUTER_COLS:] + r2


def _route(logits, tri_ref, base):
    lane = lax.broadcasted_iota(I32, logits.shape, 1)
    lane_f = lane.astype(F32)

    def top(vals):
        m = jnp.max(vals, axis=1, keepdims=True)
        return m, jnp.min(jnp.where(vals == m, lane_f, float(LANES)), axis=1, keepdims=True)

    is_group = lane < N_GROUPS
    g_max, grp = top(jnp.where(is_group, logits, NEG_LARGE))
    g_gate = 1.0 / jnp.sum(jnp.where(is_group, jnp.exp(logits - g_max), 0.0), axis=1, keepdims=True)
    first = N_GROUPS + EXPERTS_PER_GROUP * grp
    e_logits = jnp.where((lane_f >= first) & (lane_f < first + EXPERTS_PER_GROUP), logits, NEG_LARGE)
    v0, l0 = top(e_logits)
    v1, l1 = top(jnp.where(lane_f == l0, NEG_LARGE, e_logits))
    e = jnp.exp(v1 - v0)
    gate0 = g_gate / (1.0 + e)
    gate1 = gate0 * e

    hot0 = jnp.where(lane_f == l0, 1.0, 0.0)
    hot1 = jnp.where(lane_f == l1, 1.0, 0.0)
    before0 = jnp.dot(tri_ref[...], hot0.astype(BF16), preferred_element_type=F32)
    before1 = jnp.dot(tri_ref[...], hot1.astype(BF16), preferred_element_type=F32)
    n0 = jnp.sum(hot0, axis=0, keepdims=True)
    n1 = jnp.sum(hot1, axis=0, keepdims=True)
    rank0 = jnp.sum(hot0 * (before0 + base), axis=1, keepdims=True)
    rank1 = jnp.sum(hot1 * (before1 + base + n0), axis=1, keepdims=True)

    fields = [l0 - N_GROUPS, l1 - N_GROUPS, gate0, gate1, rank0, rank1]
    record = jnp.zeros(logits.shape, F32)
    for i, f in enumerate(fields):
        record = jnp.where(lane == i, f, record)
    return record, base + n0 + n1


def _finish(x, mix, is_first_step, lng_ref, lnb_ref, rw_ref, rb_ref, tri_ref,
            h_ref, hp_ref, route_ref, route_t_ref, counts_ref, cnt_ref):
    h = _layer_norm(DEEPNORM_ALPHA * x + mix, lng_ref[...], lnb_ref[...])
    h_ref[0] = h
    hp_ref[0] = _pack_bf16_pairs(h)

    @pl.when(is_first_step)
    def _():
        cnt_ref[...] = jnp.zeros_like(cnt_ref)

    record, counts = _route(_router_logits(h, rw_ref) + rb_ref[...], tri_ref, cnt_ref[...])
    route_ref[0] = record
    route_t_ref[...] = record.T[:ROUTE_ROWS]
    cnt_ref[...] = counts
    counts_ref[...] = counts


def _const_spec(shape):
    return pl.BlockSpec(shape, lambda *_: (0,) * len(shape))


def _finish_specs(b, s, d, ts):
    in_specs = [_const_spec((1, d)), _const_spec((1, d)), _const_spec((d, 2 * ROUTER_COLS)),
                _const_spec((1, ROUTER_COLS)), _const_spec((ts, ts))]
    tile = lambda w: pl.BlockSpec((1, ts, w), lambda i, j: (i, j, 0))
    out_specs = [tile(d), tile(d // 2), tile(ROUTER_COLS),
                 pl.BlockSpec((ROUTE_ROWS, ts), lambda i, j: (0, i * (s // ts) + j)),
                 _const_spec((1, ROUTER_COLS))]
    out_shape = [jax.ShapeDtypeStruct((b, s, d), F32), jax.ShapeDtypeStruct((b, s, d // 2), U32),
                 jax.ShapeDtypeStruct((b, s, ROUTER_COLS), F32),
                 jax.ShapeDtypeStruct((ROUTE_ROWS, b * s), F32),
                 jax.ShapeDtypeStruct((1, ROUTER_COLS), F32)]
    return in_specs, out_specs, out_shape, [pltpu.VMEM((1, ROUTER_COLS), F32)]


def _finish_args(ln_g, ln_b, router, ts):
    rw, rb = router
    tri = (lax.broadcasted_iota(I32, (ts, ts), 1) < lax.broadcasted_iota(I32, (ts, ts), 0)).astype(BF16)
    return ln_g.reshape(1, -1), ln_b.reshape(1, -1), rw, rb, tri


def _is_first_step():
    return (pl.program_id(0) == 0) & (pl.program_id(1) == 0)


class _ShiftedWindow:
    def __init__(self, buf, start, back):
        self.back = back
        self.window = buf[start - back:start + CONV_ROWS, :]
        self.rolled = {0: self.window}

    def rows(self, d):
        first = self.back + d
        shift = first % SUBLANES
        if shift not in self.rolled:
            n, c = self.window.shape
            tiles = pltpu.roll(self.window.reshape(n // SUBLANES, SUBLANES, c), SUBLANES - shift, axis=1)
            sub = lax.broadcasted_iota(I32, (1, SUBLANES, 1), 1)
            self.rolled[shift] = jnp.where(sub < SUBLANES - shift, tiles[:-1], tiles[1:]).reshape(n - SUBLANES, c)
        return self.rolled[shift][first - shift:first - shift + CONV_ROWS]


def _even_kernel(x_ref, win_ref, ca_ref, cbw_ref, cbb_ref, nbg_ref, nbb_ref, wout_ref,
                 lng_ref, lnb_ref, rw_ref, rb_ref, tri_ref,
                 h_ref, hp_ref, route_ref, route_t_ref, counts_ref,
                 cnt_ref, u_buf, cv_buf, g_buf, y_buf):
    s = pl.program_id(1)
    ts = x_ref.shape[1]
    x = x_ref[0]

    @pl.when(s == 0)
    def _():
        cv_buf[0:CONV_HALO] = jnp.zeros((CONV_HALO, D_CONV_A), F32)
        g_buf[0:CONV_HALO] = jnp.zeros((CONV_HALO, D_CONV_B), F32)

    @pl.when(s > 0)
    def _():
        cv_buf[0:CONV_HALO] = cv_buf[ts:ts + CONV_HALO]
        g_buf[0:CONV_HALO] = g_buf[ts:ts + CONV_HALO]

    c0 = D_CONV_A
    sec = min(EVEN_SECTION, ts)

    def project(first_row):
        rows = slice(first_row, first_row + sec)
        halo_rows = slice(CONV_HALO + first_row, CONV_HALO + first_row + sec)
        u_buf[rows] = jnp.dot(x_ref[0, rows, :].astype(BF16), win_ref[...], preferred_element_type=F32)
        cv_buf[halo_rows] = u_buf[rows, c0:2 * c0] * u_buf[rows, 2 * c0:3 * c0]
        b_val = u_buf[rows, 3 * c0:3 * c0 + D_CONV_B]
        b_gate = u_buf[rows, 3 * c0 + D_CONV_B:]
        g_buf[halo_rows] = b_val * jax.nn.sigmoid(b_gate)

    project(0)
    for r0 in range(0, ts, CONV_ROWS):
        if r0 % sec == 0 and r0 + sec < ts:
            project(r0 + sec)
        win_a = _ShiftedWindow(cv_buf, CONV_HALO + r0, SUBLANES)
        acc_a = None
        for k in range(CONV_A_WIDTH):
            term = ca_ref[k:k + 1, :] * win_a.rows(k - (CONV_A_WIDTH - 1))
            acc_a = term if acc_a is None else acc_a + term
        y_a = u_buf[r0:r0 + CONV_ROWS, 0:c0] * acc_a
        win_b = _ShiftedWindow(g_buf, CONV_HALO + r0, CONV_HALO)
        acc_b = None
        for k in range(CONV_B_WIDTH):
            term = cbw_ref[k:k + 1, :] * win_b.rows(k - (CONV_B_WIDTH - 1))
            acc_b = term if acc_b is None else acc_b + term
        gb = _layer_norm(acc_b + cbb_ref[...], nbg_ref[...], nbb_ref[...])
        y_b = gb * jax.nn.sigmoid(gb)
        y_buf[r0:r0 + CONV_ROWS, 0:c0] = y_a.astype(BF16)
        y_buf[r0:r0 + CONV_ROWS, c0:] = y_b.astype(BF16)

    mix = jnp.dot(y_buf[...], wout_ref[...], preferred_element_type=F32)
    _finish(x, mix, _is_first_step(), lng_ref, lnb_ref, rw_ref, rb_ref, tri_ref,
            h_ref, hp_ref, route_ref, route_t_ref, counts_ref, cnt_ref)


def _even_mixer(x, w_in, conv_a, conv_b_w, conv_b_bias, norm_b_g, norm_b_b, w_out, ln_g, ln_b, router):
    b, s, d = x.shape
    ts = min(ROW_TILE, s)
    row = lambda v: v.reshape(1, -1)
    f_in, f_out, f_shape, f_scratch = _finish_specs(b, s, d, ts)
    return pl.pallas_call(
        _even_kernel,
        grid=(b, s // ts),
        in_specs=[
            pl.BlockSpec((1, ts, d), lambda i, j: (i, j, 0)),
            _const_spec((d, D_IN_EVEN)),
            _const_spec((CONV_A_WIDTH, D_CONV_A)),
            _const_spec((CONV_B_WIDTH, D_CONV_B)),
            _const_spec((1, D_CONV_B)),
            _const_spec((1, D_CONV_B)),
            _const_spec((1, D_CONV_B)),
            _const_spec((D_CONV_A + D_CONV_B, d)),
            *f_in,
        ],
        out_specs=f_out,
        out_shape=f_shape,
        scratch_shapes=[
            *f_scratch,
            pltpu.VMEM((ts, D_IN_EVEN), F32),
            pltpu.VMEM((CONV_HALO + ts, D_CONV_A), F32),
            pltpu.VMEM((CONV_HALO + ts, D_CONV_B), F32),
            pltpu.VMEM((ts, D_CONV_A + D_CONV_B), BF16),
        ],
        compiler_params=pltpu.CompilerParams(
            dimension_semantics=("arbitrary", "arbitrary"), vmem_limit_bytes=VMEM_LIMIT),
    )(x, w_in.astype(BF16), conv_a, conv_b_w, row(conv_b_bias), row(norm_b_g), row(norm_b_b),
      w_out.astype(BF16), *_finish_args(ln_g, ln_b, router, ts))


def _qkv_kernel(x_ref, w_ref, o_ref):
    u = jnp.dot(x_ref[0].astype(BF16), w_ref[...], preferred_element_type=F32)
    o_ref[0, :, :D_ATTN] = (u[:, :D_ATTN] * (HEAD_DIM ** -0.5)).astype(BF16)
    o_ref[0, :, D_ATTN:] = u[:, D_ATTN:].astype(BF16)


def _qkv_proj(x, w_qkv):
    b, s, d = x.shape
    ts = min(ROW_TILE, s)
    return pl.pallas_call(
        _qkv_kernel,
        grid=(b, s // ts),
        in_specs=[pl.BlockSpec((1, ts, d), lambda i, j: (i, j, 0)), _const_spec((d, 3 * D_ATTN))],
        out_specs=pl.BlockSpec((1, ts, 3 * D_ATTN), lambda i, j: (i, j, 0)),
        out_shape=jax.ShapeDtypeStruct((b, s, 3 * D_ATTN), BF16),
        compiler_params=pltpu.CompilerParams(
            dimension_semantics=("arbitrary", "arbitrary"), vmem_limit_bytes=VMEM_LIMIT),
    )(x, w_qkv.astype(BF16))


def _attn_kernel(q_ref, k_ref, v_ref, o_ref, acc_ref, carry_ref, z_buf, sp_buf, *, blk):
    s_len = q_ref.shape[1]
    per_tile = LANES // HEAD_DIM
    lane = lax.broadcasted_iota(jnp.int32, (1, LANES), 1)
    tri = (lax.broadcasted_iota(jnp.int32, (blk, blk), 0)
           >= lax.broadcasted_iota(jnp.int32, (blk, blk), 1)).astype(BF16)
    causal = (lax.broadcasted_iota(jnp.int32, (blk, blk), 1)
              < lax.broadcasted_iota(jnp.int32, (blk, blk), 0))

    def tile_of(h):
        return slice((h // per_tile) * LANES, (h // per_tile + 1) * LANES)

    def q_block(n_back, _):
        q0 = pl.multiple_of(n_back * blk, blk)
        qms = [jnp.where((lane // HEAD_DIM) == h % per_tile, q_ref[0, pl.ds(q0, blk), tile_of(h)],
                         jnp.zeros((), BF16)) for h in range(ATTN_HEADS)]
        acc_ref[...] = jnp.zeros_like(acc_ref)
        carry_ref[...] = jnp.zeros_like(carry_ref)

        def score(m, slot, diagonal):
            k0 = pl.multiple_of(q0 - m * blk, blk)
            for h in range(ATTN_HEADS):
                kb = k_ref[0, pl.ds(k0, blk), tile_of(h)]
                z = lax.dot_general(qms[h], kb, (((1,), (1,)), ((), ())),
                                    preferred_element_type=F32)
                sp = jnp.where(z > SOFTPLUS_LINEAR, z, jnp.log(1.0 + jnp.exp(z)))
                if diagonal:
                    sp = jnp.where(causal, sp, 0.0)
                z_buf[slot, h] = z
                sp_buf[slot, h] = sp.astype(BF16)

        def apply(m, slot, diagonal):
            k0 = pl.multiple_of(q0 - m * blk, blk)
            for h in range(ATTN_HEADS):
                vb = v_ref[0, pl.ds(k0, blk), tile_of(h)]
                c = jnp.dot(sp_buf[slot, h], tri, preferred_element_type=F32)
                carry = carry_ref[h]
                a = jnp.exp(z_buf[slot, h] - c - jnp.concatenate([carry] * (blk // LANES), axis=1))
                if diagonal:
                    a = jnp.where(causal, a, 0.0)
                acc_ref[h] += jnp.dot(a.astype(BF16), vb, preferred_element_type=F32)
                carry_ref[h] = carry + jnp.broadcast_to(c[:, 0:1], carry.shape)

        @pl.when(n_back == 0)
        def _():
            score(0, 0, True)
            apply(0, 0, True)

        def live():
            return (jnp.min(carry_ref[...]) <= DECAY_STOP).astype(jnp.int32)

        @pl.when(n_back > 0)
        def _():
            score(0, 0, True)
            score(1, 1, False)
            apply(0, 0, True)
            apply(1, 1, False)

            @pl.when((n_back >= 2) & (live() == 1))
            def _():
                score(2, 0, False)

                def pair(state):
                    m, _ = state
                    score(m + 1, 1, False)
                    apply(m, 0, False)

                    @pl.when((live() == 1) & (m + 2 <= n_back))
                    def _():
                        score(m + 2, 0, False)
                        apply(m + 1, 1, False)

                    return m + 2, live()

                _, alive = lax.while_loop(lambda st: (st[0] + 1 <= n_back) & (st[1] == 1), pair,
                                          (jnp.int32(2), jnp.int32(1)))

                @pl.when((alive == 1) & (n_back % 2 == 0))
                def _():
                    apply(n_back, 0, False)

                @pl.when((alive == 1) & (n_back % 2 == 1))
                def _():
                    apply(n_back, 1, False)

        for t in range(ATTN_HEADS // per_tile):
            o = acc_ref[t * per_tile]
            for h in range(1, per_tile):
                o = jnp.where((lane // HEAD_DIM) == h, acc_ref[t * per_tile + h], o)
            o_ref[0, pl.ds(q0, blk), t * LANES:(t + 1) * LANES] = o.astype(BF16)
        return 0

    lax.fori_loop(0, s_len // blk, q_block, 0)


def _attention(qkv):
    b, s, _ = qkv.shape
    blk = min(ATTN_BLOCK, s)
    width = ATTN_HEADS * HEAD_DIM
    n_steps = D_ATTN // width
    return pl.pallas_call(
        functools.partial(_attn_kernel, blk=blk),
        grid=(b, n_steps),
        in_specs=[
            pl.BlockSpec((1, s, width), lambda i, p: (i, 0, p)),
            pl.BlockSpec((1, s, width), lambda i, p: (i, 0, n_steps + p)),
            pl.BlockSpec((1, s, width), lambda i, p: (i, 0, 2 * n_steps + p)),
        ],
        out_specs=pl.BlockSpec((1, s, width), lambda i, p: (i, 0, p)),
        out_shape=jax.ShapeDtypeStruct((b, s, D_ATTN), BF16),
        scratch_shapes=[pltpu.VMEM((ATTN_HEADS, blk, LANES), F32),
                        pltpu.VMEM((ATTN_HEADS, blk, LANES), F32),
                        pltpu.VMEM((2, ATTN_HEADS, blk, blk), F32),
                        pltpu.VMEM((2, ATTN_HEADS, blk, blk), BF16)],
        compiler_params=pltpu.CompilerParams(
            dimension_semantics=("arbitrary", "arbitrary"), vmem_limit_bytes=VMEM_LIMIT),
    )(qkv, qkv, qkv)


def _oproj_kernel(o_ref, x_ref, w_ref, lng_ref, lnb_ref, rw_ref, rb_ref, tri_ref,
                  h_ref, hp_ref, route_ref, route_t_ref, counts_ref, cnt_ref):
    mix = jnp.dot(o_ref[0], w_ref[...], preferred_element_type=F32)
    _finish(x_ref[0], mix, _is_first_step(), lng_ref, lnb_ref, rw_ref, rb_ref, tri_ref,
            h_ref, hp_ref, route_ref, route_t_ref, counts_ref, cnt_ref)


def _attn_out(o, x, w_o, ln_g, ln_b, router):
    b, s, d = x.shape
    ts = min(ROW_TILE, s)
    f_in, f_out, f_shape, f_scratch = _finish_specs(b, s, d, ts)
    return pl.pallas_call(
        _oproj_kernel,
        grid=(b, s // ts),
        in_specs=[
            pl.BlockSpec((1, ts, D_ATTN), lambda i, j: (i, j, 0)),
            pl.BlockSpec((1, ts, d), lambda i, j: (i, j, 0)),
            _const_spec((D_ATTN, d)),
            *f_in,
        ],
        out_specs=f_out,
        out_shape=f_shape,
        scratch_shapes=f_scratch,
        compiler_params=pltpu.CompilerParams(
            dimension_semantics=("arbitrary", "arbitrary"), vmem_limit_bytes=VMEM_LIMIT),
    )(o, x, w_o.astype(BF16), *_finish_args(ln_g, ln_b, router, ts))


def _dispatch_meta(route_t, counts, chunk):
    t = route_t.shape[1]
    sizes = counts[0, N_GROUPS:N_GROUPS + N_EXPERTS].astype(I32)
    ids = route_t[ROUTE_ID:ROUTE_ID + TOP_K].astype(I32)
    rank = route_t[ROUTE_RANK:ROUTE_RANK + TOP_K].astype(I32)
    padded = (sizes + chunk - 1) // chunk * chunk
    pad_end = jnp.cumsum(padded)
    pad_start = pad_end - padded
    experts = jnp.arange(N_EXPERTS, dtype=I32)
    seg_start = jnp.sum(jnp.where(ids[None] == experts[:, None, None], pad_start[:, None, None], 0), axis=0)
    dest = (seg_start + rank).astype(I32)
    n_chunks = -(-(t * TOP_K) // chunk) + N_EXPERTS
    chunk_first_row = jnp.arange(n_chunks, dtype=I32) * chunk
    chunk_expert = jnp.minimum(
        jnp.sum((chunk_first_row[:, None] >= pad_end[None, :]).astype(I32), axis=1), N_EXPERTS - 1)
    chunk_rows = jnp.clip(
        pad_start[chunk_expert] + sizes[chunk_expert] - chunk_first_row, 0, chunk).astype(I32)
    n_valid = (pad_end[-1] // chunk).astype(I32).reshape(1)
    return dest, chunk_expert, chunk_rows, n_valid, n_chunks


def _index_windows(idx):
    return jnp.pad(idx.reshape(-1, SC_WINDOW), ((0, 0), (0, LANES - SC_WINDOW)))


def _sc_row_scatter(x, dest, n_out):
    t, d = x.shape
    mesh = plsc.VectorSubcoreMesh(core_axis_name="core", subcore_axis_name="subcore")
    idx = [_index_windows(dest[k]) for k in range(TOP_K)]

    @pl.kernel(out_type=jax.ShapeDtypeStruct((n_out, d), x.dtype), mesh=mesh, scratch_types=[])
    def scatter(x_hbm, *rest):
        i_hbm, o_hbm = rest[:TOP_K], rest[TOP_K]

        def body(x_vmem, *i_vmem):
            for k in range(TOP_K):
                pltpu.sync_copy(x_vmem, o_hbm.at[i_vmem[k].at[0, pl.ds(0, SC_WINDOW)]])

        pltpu.emit_pipeline(
            body,
            grid=(t // SC_WINDOW,),
            in_specs=[pl.BlockSpec((SC_WINDOW, d), lambda i: (i, 0))]
            + [pl.BlockSpec((1, LANES), lambda i: (i, 0))] * TOP_K,
            out_specs=[],
            core_axis_name=("core", "subcore"),
            dimension_semantics=(pltpu.PARALLEL,),
        )(x_hbm, *i_hbm)

    return scatter(x, *idx)


def _sc_row_gather(x, idx):
    _, d = x.shape
    m = idx.shape[0]
    mesh = plsc.VectorSubcoreMesh(core_axis_name="core", subcore_axis_name="subcore")

    @pl.kernel(out_type=jax.ShapeDtypeStruct((m, d), x.dtype), mesh=mesh, scratch_types=[])
    def gather(x_hbm, i_hbm, o_hbm):
        def body(i_vmem, o_vmem):
            pltpu.sync_copy(x_hbm.at[i_vmem.at[0, pl.ds(0, SC_WINDOW)]], o_vmem)

        pltpu.emit_pipeline(
            body,
            grid=(m // SC_WINDOW,),
            in_specs=[pl.BlockSpec((1, LANES), lambda i: (i, 0))],
            out_specs=[pl.BlockSpec((SC_WINDOW, d), lambda i: (i, 0))],
            core_axis_name=("core", "subcore"),
            dimension_semantics=(pltpu.PARALLEL,),
        )(i_hbm, o_hbm)

    return gather(x, _index_windows(idx))


def _expert_kernel(ce_ref, rows_ref, nv_ref, x_ref, wup_ref, wdn_ref, y_ref, wup_bf, wdn_bf):
    i = pl.program_id(0)

    @pl.when(i < nv_ref[0])
    def _():
        prev = ce_ref[jnp.maximum(i - 1, 0)]

        @pl.when((i == 0) | (ce_ref[i] != prev))
        def _():
            wup_bf[...] = wup_ref[0, 0].astype(BF16)
            wdn_bf[...] = wdn_ref[0, 0].astype(BF16)

        row = lax.broadcasted_iota(I32, x_ref.shape, 0)
        packed = jnp.where(row < rows_ref[i], x_ref[...], jnp.uint32(0))
        gu = jnp.dot(_unpack_bf16_pairs(packed).astype(BF16), wup_bf[...], preferred_element_type=F32)
        gate, up = gu[:, :D_EXPERT], gu[:, D_EXPERT:]
        act = gate * jax.nn.sigmoid(gate) * up
        y = jnp.dot(act.astype(BF16), wdn_bf[...], preferred_element_type=F32)
        y_ref[...] = _pack_bf16_pairs(y)

    @pl.when(i >= nv_ref[0])
    def _():
        y_ref[...] = jnp.zeros_like(y_ref)


def _experts(x_rows, chunk_expert, chunk_rows, n_valid, n_chunks, chunk, layer, w_up, w_down):
    _, half = x_rows.shape
    d = 2 * half
    grid_spec = pltpu.PrefetchScalarGridSpec(
        num_scalar_prefetch=3,
        grid=(n_chunks,),
        in_specs=[
            pl.BlockSpec((chunk, half), lambda i, ce, rows, nv: (jnp.minimum(i, nv[0] - 1), 0)),
            pl.BlockSpec((1, 1, d, 2 * D_EXPERT), lambda i, ce, rows, nv: (layer, ce[i], 0, 0)),
            pl.BlockSpec((1, 1, D_EXPERT, d), lambda i, ce, rows, nv: (layer, ce[i], 0, 0)),
        ],
        out_specs=pl.BlockSpec((chunk, half), lambda i, ce, rows, nv: (i, 0)),
        scratch_shapes=[
            pltpu.VMEM((d, 2 * D_EXPERT), BF16),
            pltpu.VMEM((D_EXPERT, d), BF16),
        ],
    )
    return pl.pallas_call(
        _expert_kernel,
        grid_spec=grid_spec,
        out_shape=jax.ShapeDtypeStruct((n_chunks * chunk, half), U32),
        compiler_params=pltpu.CompilerParams(
            dimension_semantics=("arbitrary",), vmem_limit_bytes=VMEM_LIMIT),
    )(chunk_expert, chunk_rows, n_valid, x_rows, w_up, w_down)


def _combine_kernel(h_ref, route_ref, *rest):
    y_refs, (lng_ref, lnb_ref, o_ref) = rest[:TOP_K], rest[TOP_K:]
    gates = route_ref[:, ROUTE_GATE:ROUTE_GATE + TOP_K]
    ffn = gates[:, 0:1] * _unpack_bf16_pairs(y_refs[0][...])
    for k in range(1, TOP_K):
        ffn = ffn + gates[:, k:k + 1] * _unpack_bf16_pairs(y_refs[k][...])
    o_ref[...] = _layer_norm(DEEPNORM_ALPHA * h_ref[...] + ffn, lng_ref[...], lnb_ref[...])


def _combine(h2d, route, y_sel, ln_g, ln_b):
    t, d = h2d.shape
    tm = min(ROW_TILE, t)
    row = lambda v: v.reshape(1, -1)
    y_specs = [pl.BlockSpec((tm, d // 2), functools.partial(lambda i, k: (k * (t // tm) + i, 0), k=k))
               for k in range(TOP_K)]
    return pl.pallas_call(
        _combine_kernel,
        grid=(t // tm,),
        in_specs=[pl.BlockSpec((tm, d), lambda i: (i, 0)),
                  pl.BlockSpec((tm, ROUTER_COLS), lambda i: (i, 0)),
                  *y_specs, _const_spec((1, d)), _const_spec((1, d))],
        out_specs=pl.BlockSpec((tm, d), lambda i: (i, 0)),
        out_shape=jax.ShapeDtypeStruct((t, d), F32),
        compiler_params=pltpu.CompilerParams(
            dimension_semantics=("arbitrary",), vmem_limit_bytes=VMEM_LIMIT),
    )(h2d, route, *([y_sel] * TOP_K), row(ln_g), row(ln_b))


def _moe_block(h, h_packed, route, route_t, counts, layer, w_up, w_down, ln_g, ln_b):
    b, s, d = h.shape
    t = b * s
    chunk = min(EXPERT_CHUNK, t)
    dest, chunk_expert, chunk_rows, n_valid, n_chunks = _dispatch_meta(route_t, counts, chunk)
    x_rows = _sc_row_scatter(h_packed.reshape(t, d // 2), dest, n_chunks * chunk)
    y_rows = _experts(x_rows, chunk_expert, chunk_rows, n_valid, n_chunks, chunk, layer, w_up, w_down)
    y_sel = _sc_row_gather(y_rows, dest.reshape(-1))
    return _combine(h.reshape(t, d), route.reshape(t, ROUTER_COLS), y_sel, ln_g, ln_b).reshape(b, s, d)


def _router_params(rg_w, rg_b, re_w, re_b):
    w = jnp.concatenate([rg_w, re_w], axis=1)
    w = jnp.pad(w, ((0, 0), (0, ROUTER_COLS - w.shape[1])))
    hi = w.astype(BF16)
    lo = (w - hi.astype(F32)).astype(BF16)
    bias = jnp.concatenate([rg_b, re_b])
    bias = jnp.pad(bias, (0, ROUTER_COLS - bias.shape[0])).reshape(1, ROUTER_COLS)
    return jnp.concatenate([hi, lo], axis=1), bias


def kernel(x, even_w_in, even_conv_a, even_conv_b_w, even_conv_b_bias, even_norm_b_g,
           even_norm_b_b, even_w_out, odd_w_qkv, odd_w_o, ln_mix_g, ln_mix_b, ln_ffn_g,
           ln_ffn_b, router_group_w, router_group_b, router_expert_w, router_expert_b,
           expert_w_up, expert_w_down):
    n_streams = BATCH_STREAMS if x.shape[0] % BATCH_STREAMS == 0 else 1
    streams = jnp.split(x, n_streams, axis=0)
    for layer in range(DEPTH):
        i = layer // 2
        router = _router_params(router_group_w[layer], router_group_b[layer],
                                router_expert_w[layer], router_expert_b[layer])
        if layer % 2 == 0:
            mixed = [_even_mixer(
                xs, even_w_in[i], even_conv_a[i], even_conv_b_w[i], even_conv_b_bias[i],
                even_norm_b_g[i], even_norm_b_b[i], even_w_out[i], ln_mix_g[layer], ln_mix_b[layer],
                router) for xs in streams]
        else:
            mixed = [_attn_out(_attention(_qkv_proj(xs, odd_w_qkv[i])), xs, odd_w_o[i],
                               ln_mix_g[layer], ln_mix_b[layer], router) for xs in streams]
        streams = [_moe_block(*m, layer, expert_w_up, expert_w_down, ln_ffn_g[layer], ln_ffn_b[layer])
                   for m in mixed]
    return jnp.concatenate(streams, axis=0)
```

```python
import functools

import jax
import jax.numpy as jnp
from jax import lax
from jax.experimental import pallas as pl
from jax.experimental.pallas import tpu as pltpu
from jax.experimental.pallas import tpu_sc as plsc

F32 = jnp.float32
BF16 = jnp.bfloat16
U32 = jnp.uint32
I32 = jnp.int32

D_MODEL = 1024
DEPTH = 4
D_CONV_A = 512
CONV_A_WIDTH = 3
D_CONV_B = 512
CONV_B_WIDTH = 31
N_HEADS = 16
HEAD_DIM = 64
D_ATTN = N_HEADS * HEAD_DIM
N_GROUPS = 4
EXPERTS_PER_GROUP = 8
N_EXPERTS = N_GROUPS * EXPERTS_PER_GROUP
TOP_K = 2
D_EXPERT = 512
LN_EPS = 1e-5
DEEPNORM_ALPHA = (2 * DEPTH) ** 0.25
D_IN_EVEN = 3 * D_CONV_A + 2 * D_CONV_B

LANES = 128
SUBLANES = 8
ROW_TILE = 512
COMBINE_TILE = 1024
CONV_HALO = 32
CONV_ROWS = 128
EVEN_SECTION = 128
ATTN_BLOCK = 256
ATTN_HEADS = 8
EXPERT_CHUNK = 512
SC_WINDOW = 64
BATCH_STREAMS = 1
ROUTER_COLS = LANES
VMEM_LIMIT = 56 * 1024 * 1024
NEG_LARGE = -1e30
SOFTPLUS_LINEAR = 80.0
DECAY_STOP = 104.0

ROUTE_ID, ROUTE_GATE, ROUTE_RANK = 0, TOP_K, 2 * TOP_K
ROUTE_ROWS = 8


def _layer_norm(v, g, b):
    mu = jnp.mean(v, axis=-1, keepdims=True)
    c = v - mu
    var = jnp.mean(c * c, axis=-1, keepdims=True)
    return c * lax.rsqrt(var + LN_EPS) * g + b


def _pack_bf16_pairs(v):
    half = v.shape[1] // 2
    hi = lax.bitcast_convert_type(v[:, :half].astype(BF16).astype(F32), U32)
    lo = lax.bitcast_convert_type(v[:, half:].astype(BF16).astype(F32), U32)
    return hi | (lo >> 16)


def _unpack_bf16_pairs(w):
    hi = lax.bitcast_convert_type(w & jnp.uint32(0xFFFF0000), F32)
    lo = lax.bitcast_convert_type(w << 16, F32)
    return jnp.concatenate([hi, lo], axis=1)


def _router_logits(h, rw_ref):
    hi = h.astype(BF16)
    lo = (h - hi.astype(F32)).astype(BF16)
    r1 = jnp.dot(hi, rw_ref[...], preferred_element_type=F32)
    r2 = jnp.dot(lo, rw_ref[:, :ROUTER_COLS], preferred_element_type=F32)
    return r1[:, :ROUTER_COLS] + r1[:, ROUTER_COLS:] + r2


def _route(logits, tri_ref, base):
    lane = lax.broadcasted_iota(I32, logits.shape, 1)
    lane_f = lane.astype(F32)

    def top(vals):
        m = jnp.max(vals, axis=1, keepdims=True)
        return m, jnp.min(jnp.where(vals == m, lane_f, float(LANES)), axis=1, keepdims=True)

    is_group = lane < N_GROUPS
    g_max, grp = top(jnp.where(is_group, logits, NEG_LARGE))
    g_gate = 1.0 / jnp.sum(jnp.where(is_group, jnp.exp(logits - g_max), 0.0), axis=1, keepdims=True)
    first = N_GROUPS + EXPERTS_PER_GROUP * grp
    e_logits = jnp.where((lane_f >= first) & (lane_f < first + EXPERTS_PER_GROUP), logits, NEG_LARGE)
    v0, l0 = top(e_logits)
    v1, l1 = top(jnp.where(lane_f == l0, NEG_LARGE, e_logits))
    e = jnp.exp(v1 - v0)
    gate0 = g_gate / (1.0 + e)
    gate1 = gate0 * e

    hot0 = jnp.where(lane_f == l0, 1.0, 0.0)
    hot1 = jnp.where(lane_f == l1, 1.0, 0.0)
    before0 = jnp.dot(tri_ref[...], hot0.astype(BF16), preferred_element_type=F32)
    before1 = jnp.dot(tri_ref[...], hot1.astype(BF16), preferred_element_type=F32)
    n0 = jnp.sum(hot0, axis=0, keepdims=True)
    n1 = jnp.sum(hot1, axis=0, keepdims=True)
    rank0 = jnp.sum(hot0 * (before0 + base), axis=1, keepdims=True)
    rank1 = jnp.sum(hot1 * (before1 + base + n0), axis=1, keepdims=True)

    fields = [l0 - N_GROUPS, l1 - N_GROUPS, gate0, gate1, rank0, rank1]
    record = jnp.zeros(logits.shape, F32)
    for i, f in enumerate(fields):
        record = jnp.where(lane == i, f, record)
    return record, base + n0 + n1


def _finish(x, mix, is_first_step, lng_ref, lnb_ref, rw_ref, rb_ref, tri_ref,
            h_ref, hp_ref, route_ref, route_t_ref, counts_ref, cnt_ref):
    h = _layer_norm(DEEPNORM_ALPHA * x + mix, lng_ref[...], lnb_ref[...])
    h_ref[0] = h
    hp_ref[0] = _pack_bf16_pairs(h)

    @pl.when(is_first_step)
    def _():
        cnt_ref[...] = jnp.zeros_like(cnt_ref)

    record, counts = _route(_router_logits(h, rw_ref) + rb_ref[...], tri_ref, cnt_ref[...])
    route_ref[0] = record
    route_t_ref[...] = record.T[:ROUTE_ROWS]
    cnt_ref[...] = counts
    counts_ref[...] = counts


def _const_spec(shape):
    return pl.BlockSpec(shape, lambda *_: (0,) * len(shape))


def _finish_specs(b, s, d, ts):
    in_specs = [_const_spec((1, d)), _const_spec((1, d)), _const_spec((d, 2 * ROUTER_COLS)),
                _const_spec((1, ROUTER_COLS)), _const_spec((ts, ts))]
    tile = lambda w: pl.BlockSpec((1, ts, w), lambda i, j: (i, j, 0))
    out_specs = [tile(d), tile(d // 2), tile(ROUTER_COLS),
                 pl.BlockSpec((ROUTE_ROWS, ts), lambda i, j: (0, i * (s // ts) + j)),
                 _const_spec((1, ROUTER_COLS))]
    out_shape = [jax.ShapeDtypeStruct((b, s, d), F32), jax.ShapeDtypeStruct((b, s, d // 2), U32),
                 jax.ShapeDtypeStruct((b, s, ROUTER_COLS), F32),
                 jax.ShapeDtypeStruct((ROUTE_ROWS, b * s), F32),
                 jax.ShapeDtypeStruct((1, ROUTER_COLS), F32)]
    return in_specs, out_specs, out_shape, [pltpu.VMEM((1, ROUTER_COLS), F32)]


def _finish_args(ln_g, ln_b, router, ts):
    rw, rb = router
    tri = (lax.broadcasted_iota(I32, (ts, ts), 1) < lax.broadcasted_iota(I32, (ts, ts), 0)).astype(BF16)
    return ln_g.reshape(1, -1), ln_b.reshape(1, -1), rw, rb, tri


def _is_first_step():
    return (pl.program_id(0) == 0) & (pl.program_id(1) == 0)


class _ShiftedWindow:
    def __init__(self, buf, start, back):
        self.back = back
        self.window = buf[start - back:start + CONV_ROWS, :]
        self.rolled = {0: self.window}

    def rows(self, d):
        first = self.back + d
        shift = first % SUBLANES
        if shift not in self.rolled:
            n, c = self.window.shape
            tiles = pltpu.roll(self.window.reshape(n // SUBLANES, SUBLANES, c), SUBLANES - shift, axis=1)
            sub = lax.broadcasted_iota(I32, (1, SUBLANES, 1), 1)
            self.rolled[shift] = jnp.where(sub < SUBLANES - shift, tiles[:-1], tiles[1:]).reshape(n - SUBLANES, c)
        return self.rolled[shift][first - shift:first - shift + CONV_ROWS]


def _even_kernel(x_ref, win_ref, ca_ref, cbw_ref, cbb_ref, nbg_ref, nbb_ref, wout_ref,
                 lng_ref, lnb_ref, rw_ref, rb_ref, tri_ref,
                 h_ref, hp_ref, route_ref, route_t_ref, counts_ref,
                 cnt_ref, u_buf, cv_buf, g_buf, y_buf):
    s = pl.program_id(1)
    ts = x_ref.shape[1]
    x = x_ref[0]

    @pl.when(s == 0)
    def _():
        cv_buf[0:CONV_HALO] = jnp.zeros((CONV_HALO, D_CONV_A), F32)
        g_buf[0:CONV_HALO] = jnp.zeros((CONV_HALO, D_CONV_B), F32)

    @pl.when(s > 0)
    def _():
        cv_buf[0:CONV_HALO] = cv_buf[ts:ts + CONV_HALO]
        g_buf[0:CONV_HALO] = g_buf[ts:ts + CONV_HALO]

    c0 = D_CONV_A
    sec = min(EVEN_SECTION, ts)

    def project(first_row):
        rows = slice(first_row, first_row + sec)
        halo_rows = slice(CONV_HALO + first_row, CONV_HALO + first_row + sec)
        u_buf[rows] = jnp.dot(x_ref[0, rows, :].astype(BF16), win_ref[...], preferred_element_type=F32)
        cv_buf[halo_rows] = u_buf[rows, c0:2 * c0] * u_buf[rows, 2 * c0:3 * c0]
        b_val = u_buf[rows, 3 * c0:3 * c0 + D_CONV_B]
        b_gate = u_buf[rows, 3 * c0 + D_CONV_B:]
        g_buf[halo_rows] = b_val * jax.nn.sigmoid(b_gate)

    project(0)
    for r0 in range(0, ts, CONV_ROWS):
        if r0 % sec == 0 and r0 + sec < ts:
            project(r0 + sec)
        win_a = _ShiftedWindow(cv_buf, CONV_HALO + r0, SUBLANES)
        acc_a = None
        for k in range(CONV_A_WIDTH):
            term = ca_ref[k:k + 1, :] * win_a.rows(k - (CONV_A_WIDTH - 1))
            acc_a = term if acc_a is None else acc_a + term
        y_a = u_buf[r0:r0 + CONV_ROWS, 0:c0] * acc_a
        win_b = _ShiftedWindow(g_buf, CONV_HALO + r0, CONV_HALO)
        acc_b = None
        for k in range(CONV_B_WIDTH):
            term = cbw_ref[k:k + 1, :] * win_b.rows(k - (CONV_B_WIDTH - 1))
            acc_b = term if acc_b is None else acc_b + term
        gb = _layer_norm(acc_b + cbb_ref[...], nbg_ref[...], nbb_ref[...])
        y_b = gb * jax.nn.sigmoid(gb)
        y_buf[r0:r0 + CONV_ROWS, 0:c0] = y_a.astype(BF16)
        y_buf[r0:r0 + CONV_ROWS, c0:] = y_b.astype(BF16)

    mix = jnp.dot(y_buf[...], wout_ref[...], preferred_element_type=F32)
    _finish(x, mix, _is_first_step(), lng_ref, lnb_ref, rw_ref, rb_ref, tri_ref,
            h_ref, hp_ref, route_ref, route_t_ref, counts_ref, cnt_ref)


def _even_mixer(x, w_in, conv_a, conv_b_w, conv_b_bias, norm_b_g, norm_b_b, w_out, ln_g, ln_b, router):
    b, s, d = x.shape
    ts = min(ROW_TILE, s)
    row = lambda v: v.reshape(1, -1)
    f_in, f_out, f_shape, f_scratch = _finish_specs(b, s, d, ts)
    return pl.pallas_call(
        _even_kernel,
        grid=(b, s // ts),
        in_specs=[
            pl.BlockSpec((1, ts, d), lambda i, j: (i, j, 0)),
            _const_spec((d, D_IN_EVEN)),
            _const_spec((CONV_A_WIDTH, D_CONV_A)),
            _const_spec((CONV_B_WIDTH, D_CONV_B)),
            _const_spec((1, D_CONV_B)),
            _const_spec((1, D_CONV_B)),
            _const_spec((1, D_CONV_B)),
            _const_spec((D_CONV_A + D_CONV_B, d)),
            *f_in,
        ],
        out_specs=f_out,
        out_shape=f_shape,
        scratch_shapes=[
            *f_scratch,
            pltpu.VMEM((ts, D_IN_EVEN), F32),
            pltpu.VMEM((CONV_HALO + ts, D_CONV_A), F32),
            pltpu.VMEM((CONV_HALO + ts, D_CONV_B), F32),
            pltpu.VMEM((ts, D_CONV_A + D_CONV_B), BF16),
        ],
        compiler_params=pltpu.CompilerParams(
            dimension_semantics=("arbitrary", "arbitrary"), vmem_limit_bytes=VMEM_LIMIT),
    )(x, w_in.astype(BF16), conv_a, conv_b_w, row(conv_b_bias), row(norm_b_g), row(norm_b_b),
      w_out.astype(BF16), *_finish_args(ln_g, ln_b, router, ts))


def _qkv_kernel(x_ref, w_ref, o_ref):
    u = jnp.dot(x_ref[0].astype(BF16), w_ref[...], preferred_element_type=F32)
    o_ref[0, :, :D_ATTN] = (u[:, :D_ATTN] * (HEAD_DIM ** -0.5)).astype(BF16)
    o_ref[0, :, D_ATTN:] = u[:, D_ATTN:].astype(BF16)


def _qkv_proj(x, w_qkv):
    b, s, d = x.shape
    ts = min(ROW_TILE, s)
    return pl.pallas_call(
        _qkv_kernel,
        grid=(b, s // ts),
        in_specs=[pl.BlockSpec((1, ts, d), lambda i, j: (i, j, 0)), _const_spec((d, 3 * D_ATTN))],
        out_specs=pl.BlockSpec((1, ts, 3 * D_ATTN), lambda i, j: (i, j, 0)),
        out_shape=jax.ShapeDtypeStruct((b, s, 3 * D_ATTN), BF16),
        compiler_params=pltpu.CompilerParams(
            dimension_semantics=("arbitrary", "arbitrary"), vmem_limit_bytes=VMEM_LIMIT),
    )(x, w_qkv.astype(BF16))


def _attn_kernel(q_ref, k_ref, v_ref, o_ref, acc_ref, carry_ref, z_buf, sp_buf, *, blk):
    s_len = q_ref.shape[1]
    per_tile = LANES // HEAD_DIM
    lane = lax.broadcasted_iota(jnp.int32, (1, LANES), 1)
    tri = (lax.broadcasted_iota(jnp.int32, (blk, blk), 0)
           >= lax.broadcasted_iota(jnp.int32, (blk, blk), 1)).astype(BF16)
    causal = (lax.broadcasted_iota(jnp.int32, (blk, blk), 1)
              < lax.broadcasted_iota(jnp.int32, (blk, blk), 0))

    def tile_of(h):
        return slice((h // per_tile) * LANES, (h // per_tile + 1) * LANES)

    def q_block(n_back, _):
        q0 = pl.multiple_of(n_back * blk, blk)
        qms = [jnp.where((lane // HEAD_DIM) == h % per_tile, q_ref[0, pl.ds(q0, blk), tile_of(h)],
                         jnp.zeros((), BF16)) for h in range(ATTN_HEADS)]
        acc_ref[...] = jnp.zeros_like(acc_ref)
        carry_ref[...] = jnp.zeros_like(carry_ref)

        def score(m, slot, diagonal):
            k0 = pl.multiple_of(q0 - m * blk, blk)
            for h in range(ATTN_HEADS):
                kb = k_ref[0, pl.ds(k0, blk), tile_of(h)]
                z = lax.dot_general(qms[h], kb, (((1,), (1,)), ((), ())),
                                    preferred_element_type=F32)
                sp = jnp.where(z > SOFTPLUS_LINEAR, z, jnp.log(1.0 + jnp.exp(z)))
                if diagonal:
                    sp = jnp.where(causal, sp, 0.0)
                z_buf[slot, h] = z
                sp_buf[slot, h] = sp.astype(BF16)

        def apply(m, slot, diagonal):
            k0 = pl.multiple_of(q0 - m * blk, blk)
            for h in range(ATTN_HEADS):
                vb = v_ref[0, pl.ds(k0, blk), tile_of(h)]
                c = jnp.dot(sp_buf[slot, h], tri, preferred_element_type=F32)
                carry = carry_ref[h]
                a = jnp.exp(z_buf[slot, h] - c - jnp.concatenate([carry] * (blk // LANES), axis=1))
                if diagonal:
                    a = jnp.where(causal, a, 0.0)
                acc_ref[h] += jnp.dot(a.astype(BF16), vb, preferred_element_type=F32)
                carry_ref[h] = carry + jnp.broadcast_to(c[:, 0:1], carry.shape)

        @pl.when(n_back == 0)
        def _():
            score(0, 0, True)
            apply(0, 0, True)

        def live():
            return (jnp.min(carry_ref[...]) <= DECAY_STOP).astype(jnp.int32)

        @pl.when(n_back > 0)
        def _():
            score(0, 0, True)
            score(1, 1, False)
            apply(0, 0, True)
            apply(1, 1, False)

            @pl.when((n_back >= 2) & (live() == 1))
            def _():
                score(2, 0, False)

                def pair(state):
                    m, _ = state
                    score(m + 1, 1, False)
                    apply(m, 0, False)

                    @pl.when((live() == 1) & (m + 2 <= n_back))
                    def _():
                        score(m + 2, 0, False)
                        apply(m + 1, 1, False)

                    return m + 2, live()

                _, alive = lax.while_loop(lambda st: (st[0] + 1 <= n_back) & (st[1] == 1), pair,
                                          (jnp.int32(2), jnp.int32(1)))

                @pl.when((alive == 1) & (n_back % 2 == 0))
                def _():
                    apply(n_back, 0, False)

                @pl.when((alive == 1) & (n_back % 2 == 1))
                def _():
                    apply(n_back, 1, False)

        for t in range(ATTN_HEADS // per_tile):
            o = acc_ref[t * per_tile]
            for h in range(1, per_tile):
                o = jnp.where((lane // HEAD_DIM) == h, acc_ref[t * per_tile + h], o)
            o_ref[0, pl.ds(q0, blk), t * LANES:(t + 1) * LANES] = o.astype(BF16)
        return 0

    lax.fori_loop(0, s_len // blk, q_block, 0)


def _attention(qkv):
    b, s, _ = qkv.shape
    blk = min(ATTN_BLOCK, s)
    width = ATTN_HEADS * HEAD_DIM
    n_steps = D_ATTN // width
    return pl.pallas_call(
        functools.partial(_attn_kernel, blk=blk),
        grid=(b, n_steps),
        in_specs=[
            pl.BlockSpec((1, s, width), lambda i, p: (i, 0, p)),
            pl.BlockSpec((1, s, width), lambda i, p: (i, 0, n_steps + p)),
            pl.BlockSpec((1, s, width), lambda i, p: (i, 0, 2 * n_steps + p)),
        ],
        out_specs=pl.BlockSpec((1, s, width), lambda i, p: (i, 0, p)),
        out_shape=jax.ShapeDtypeStruct((b, s, D_ATTN), BF16),
        scratch_shapes=[pltpu.VMEM((ATTN_HEADS, blk, LANES), F32),
                        pltpu.VMEM((ATTN_HEADS, blk, LANES), F32),
                        pltpu.VMEM((2, ATTN_HEADS, blk, blk), F32),
                        pltpu.VMEM((2, ATTN_HEADS, blk, blk), BF16)],
        compiler_params=pltpu.CompilerParams(
            dimension_semantics=("arbitrary", "arbitrary"), vmem_limit_bytes=VMEM_LIMIT),
    )(qkv, qkv, qkv)


def _oproj_kernel(o_ref, x_ref, w_ref, lng_ref, lnb_ref, rw_ref, rb_ref, tri_ref,
                  h_ref, hp_ref, route_ref, route_t_ref, counts_ref, cnt_ref):
    mix = jnp.dot(o_ref[0], w_ref[...], preferred_element_type=F32)
    _finish(x_ref[0], mix, _is_first_step(), lng_ref, lnb_ref, rw_ref, rb_ref, tri_ref,
            h_ref, hp_ref, route_ref, route_t_ref, counts_ref, cnt_ref)


def _attn_out(o, x, w_o, ln_g, ln_b, router):
    b, s, d = x.shape
    ts = min(ROW_TILE, s)
    f_in, f_out, f_shape, f_scratch = _finish_specs(b, s, d, ts)
    return pl.pallas_call(
        _oproj_kernel,
        grid=(b, s // ts),
        in_specs=[
            pl.BlockSpec((1, ts, D_ATTN), lambda i, j: (i, j, 0)),
            pl.BlockSpec((1, ts, d), lambda i, j: (i, j, 0)),
            _const_spec((D_ATTN, d)),
            *f_in,
        ],
        out_specs=f_out,
        out_shape=f_shape,
        scratch_shapes=f_scratch,
        compiler_params=pltpu.CompilerParams(
            dimension_semantics=("arbitrary", "arbitrary"), vmem_limit_bytes=VMEM_LIMIT),
    )(o, x, w_o.astype(BF16), *_finish_args(ln_g, ln_b, router, ts))


def _dispatch_meta(route_t, counts, chunk):
    t = route_t.shape[1]
    sizes = counts[0, N_GROUPS:N_GROUPS + N_EXPERTS].astype(I32)
    ids = route_t[ROUTE_ID:ROUTE_ID + TOP_K].astype(I32)
    rank = route_t[ROUTE_RANK:ROUTE_RANK + TOP_K].astype(I32)
    padded = (sizes + chunk - 1) // chunk * chunk
    pad_end = jnp.cumsum(padded)
    pad_start = pad_end - padded
    experts = jnp.arange(N_EXPERTS, dtype=I32)
    seg_start = jnp.sum(jnp.where(ids[None] == experts[:, None, None], pad_start[:, None, None], 0), axis=0)
    dest = (seg_start + rank).astype(I32)
    n_chunks = -(-(t * TOP_K) // chunk) + N_EXPERTS
    chunk_first_row = jnp.arange(n_chunks, dtype=I32) * chunk
    chunk_expert = jnp.minimum(
        jnp.sum((chunk_first_row[:, None] >= pad_end[None, :]).astype(I32), axis=1), N_EXPERTS - 1)
    chunk_rows = jnp.clip(
        pad_start[chunk_expert] + sizes[chunk_expert] - chunk_first_row, 0, chunk).astype(I32)
    n_valid = (pad_end[-1] // chunk).astype(I32).reshape(1)
    return dest, chunk_expert, chunk_rows, n_valid, n_chunks


def _index_windows(idx):
    return jnp.pad(idx.reshape(-1, SC_WINDOW), ((0, 0), (0, LANES - SC_WINDOW)))


def _sc_row_scatter(x, dest, n_out):
    t, d = x.shape
    mesh = plsc.VectorSubcoreMesh(core_axis_name="core", subcore_axis_name="subcore")
    idx = [_index_windows(dest[k]) for k in range(TOP_K)]

    @pl.kernel(out_type=jax.ShapeDtypeStruct((n_out, d), x.dtype), mesh=mesh, scratch_types=[])
    def scatter(x_hbm, *rest):
        i_hbm, o_hbm = rest[:TOP_K], rest[TOP_K]

        def body(x_vmem, *i_vmem):
            for k in range(TOP_K):
                pltpu.sync_copy(x_vmem, o_hbm.at[i_vmem[k].at[0, pl.ds(0, SC_WINDOW)]])

        pltpu.emit_pipeline(
            body,
            grid=(t // SC_WINDOW,),
            in_specs=[pl.BlockSpec((SC_WINDOW, d), lambda i: (i, 0))]
            + [pl.BlockSpec((1, LANES), lambda i: (i, 0))] * TOP_K,
            out_specs=[],
            core_axis_name=("core", "subcore"),
            dimension_semantics=(pltpu.PARALLEL,),
        )(x_hbm, *i_hbm)

    return scatter(x, *idx)


def _sc_row_gather(x, idx):
    _, d = x.shape
    m = idx.shape[0]
    mesh = plsc.VectorSubcoreMesh(core_axis_name="core", subcore_axis_name="subcore")

    @pl.kernel(out_type=jax.ShapeDtypeStruct((m, d), x.dtype), mesh=mesh, scratch_types=[])
    def gather(x_hbm, i_hbm, o_hbm):
        def body(i_vmem, o_vmem):
            pltpu.sync_copy(x_hbm.at[i_vmem.at[0, pl.ds(0, SC_WINDOW)]], o_vmem)

        pltpu.emit_pipeline(
            body,
            grid=(m // SC_WINDOW,),
            in_specs=[pl.BlockSpec((1, LANES), lambda i: (i, 0))],
            out_specs=[pl.BlockSpec((SC_WINDOW, d), lambda i: (i, 0))],
            core_axis_name=("core", "subcore"),
            dimension_semantics=(pltpu.PARALLEL,),
        )(i_hbm, o_hbm)

    return gather(x, _index_windows(idx))


def _expert_kernel(ce_ref, rows_ref, nv_ref, x_ref, wup_ref, wdn_ref, y_ref, wup_bf, wdn_bf):
    i = pl.program_id(0)

    @pl.when(i < nv_ref[0])
    def _():
        prev = ce_ref[jnp.maximum(i - 1, 0)]

        @pl.when((i == 0) | (ce_ref[i] != prev))
        def _():
            wup_bf[...] = wup_ref[0, 0].astype(BF16)
            wdn_bf[...] = wdn_ref[0, 0].astype(BF16)

        row = lax.broadcasted_iota(I32, x_ref.shape, 0)
        packed = jnp.where(row < rows_ref[i], x_ref[...], jnp.uint32(0))
        gu = jnp.dot(_unpack_bf16_pairs(packed).astype(BF16), wup_bf[...], preferred_element_type=F32)
        gate, up = gu[:, :D_EXPERT], gu[:, D_EXPERT:]
        act = gate * jax.nn.sigmoid(gate) * up
        y = jnp.dot(act.astype(BF16), wdn_bf[...], preferred_element_type=F32)
        y_ref[...] = _pack_bf16_pairs(y)

    @pl.when(i >= nv_ref[0])
    def _():
        y_ref[...] = jnp.zeros_like(y_ref)


def _experts(x_rows, chunk_expert, chunk_rows, n_valid, n_chunks, chunk, layer, w_up, w_down):
    _, half = x_rows.shape
    d = 2 * half
    grid_spec = pltpu.PrefetchScalarGridSpec(
        num_scalar_prefetch=3,
        grid=(n_chunks,),
        in_specs=[
            pl.BlockSpec((chunk, half), lambda i, ce, rows, nv: (jnp.minimum(i, nv[0] - 1), 0)),
            pl.BlockSpec((1, 1, d, 2 * D_EXPERT), lambda i, ce, rows, nv: (layer, ce[i], 0, 0)),
            pl.BlockSpec((1, 1, D_EXPERT, d), lambda i, ce, rows, nv: (layer, ce[i], 0, 0)),
        ],
        out_specs=pl.BlockSpec((chunk, half), lambda i, ce, rows, nv: (i, 0)),
        scratch_shapes=[
            pltpu.VMEM((d, 2 * D_EXPERT), BF16),
            pltpu.VMEM((D_EXPERT, d), BF16),
        ],
    )
    return pl.pallas_call(
        _expert_kernel,
        grid_spec=grid_spec,
        out_shape=jax.ShapeDtypeStruct((n_chunks * chunk, half), U32),
        compiler_params=pltpu.CompilerParams(
            dimension_semantics=("arbitrary",), vmem_limit_bytes=VMEM_LIMIT),
    )(chunk_expert, chunk_rows, n_valid, x_rows, w_up, w_down)


def _combine_kernel(h_ref, route_ref, *rest):
    y_refs, (lng_ref, lnb_ref, o_ref) = rest[:TOP_K], rest[TOP_K:]
    gates = route_ref[:, ROUTE_GATE:ROUTE_GATE + TOP_K]
    ffn = gates[:, 0:1] * _unpack_bf16_pairs(y_refs[0][...])
    for k in range(1, TOP_K):
        ffn = ffn + gates[:, k:k + 1] * _unpack_bf16_pairs(y_refs[k][...])
    o_ref[...] = _layer_norm(DEEPNORM_ALPHA * h_ref[...] + ffn, lng_ref[...], lnb_ref[...])


def _combine(h2d, route, y_sel, ln_g, ln_b):
    t, d = h2d.shape
    tm = min(COMBINE_TILE, t)
    row = lambda v: v.reshape(1, -1)
    y_specs = [pl.BlockSpec((tm, d // 2), functools.partial(lambda i, k: (k * (t // tm) + i, 0), k=k))
               for k in range(TOP_K)]
    return pl.pallas_call(
        _combine_kernel,
        grid=(t // tm,),
        in_specs=[pl.BlockSpec((tm, d), lambda i: (i, 0)),
                  pl.BlockSpec((tm, ROUTER_COLS), lambda i: (i, 0)),
                  *y_specs, _const_spec((1, d)), _const_spec((1, d))],
        out_specs=pl.BlockSpec((tm, d), lambda i: (i, 0)),
        out_shape=jax.ShapeDtypeStruct((t, d), F32),
        compiler_params=pltpu.CompilerParams(
            dimension_semantics=("arbitrary",), vmem_limit_bytes=VMEM_LIMIT),
    )(h2d, route, *([y_sel] * TOP_K), row(ln_g), row(ln_b))


def _moe_block(h, h_packed, route, route_t, counts, layer, w_up, w_down, ln_g, ln_b):
    b, s, d = h.shape
    t = b * s
    chunk = min(EXPERT_CHUNK, t)
    dest, chunk_expert, chunk_rows, n_valid, n_chunks = _dispatch_meta(route_t, counts, chunk)
    x_rows = _sc_row_scatter(h_packed.reshape(t, d // 2), dest, n_chunks * chunk)
    y_rows = _experts(x_rows, chunk_expert, chunk_rows, n_valid, n_chunks, chunk, layer, w_up, w_down)
    y_sel = _sc_row_gather(y_rows, dest.reshape(-1))
    return _combine(h.reshape(t, d), route.reshape(t, ROUTER_COLS), y_sel, ln_g, ln_b).reshape(b, s, d)


def _router_params(rg_w, rg_b, re_w, re_b):
    w = jnp.concatenate([rg_w, re_w], axis=1)
    w = jnp.pad(w, ((0, 0), (0, ROUTER_COLS - w.shape[1])))
    hi = w.astype(BF16)
    lo = (w - hi.astype(F32)).astype(BF16)
    bias = jnp.concatenate([rg_b, re_b])
    bias = jnp.pad(bias, (0, ROUTER_COLS - bias.shape[0])).reshape(1, ROUTER_COLS)
    return jnp.concatenate([hi, lo], axis=1), bias


def kernel(x, even_w_in, even_conv_a, even_conv_b_w, even_conv_b_bias, even_norm_b_g,
           even_norm_b_b, even_w_out, odd_w_qkv, odd_w_o, ln_mix_g, ln_mix_b, ln_ffn_g,
           ln_ffn_b, router_group_w, router_group_b, router_expert_w, router_expert_b,
           expert_w_up, expert_w_down):
    n_streams = BATCH_STREAMS if x.shape[0] % BATCH_STREAMS == 0 else 1
    streams = jnp.split(x, n_streams, axis=0)
    for layer in range(DEPTH):
        i = layer // 2
        router = _router_params(router_group_w[layer], router_group_b[layer],
                                router_expert_w[layer], router_expert_b[layer])
        if layer % 2 == 0:
            mixed = [_even_mixer(
                xs, even_w_in[i], even_conv_a[i], even_conv_b_w[i], even_conv_b_bias[i],
                even_norm_b_g[i], even_norm_b_b[i], even_w_out[i], ln_mix_g[layer], ln_mix_b[layer],
                router) for xs in streams]
        else:
            mixed = [_attn_out(_attention(_qkv_proj(xs, odd_w_qkv[i])), xs, odd_w_o[i],
                               ln_mix_g[layer], ln_mix_b[layer], router) for xs in streams]
        streams = [_moe_block(*m, layer, expert_w_up, expert_w_down, ln_ffn_g[layer], ln_ffn_b[layer])
                   for m in mixed]
    return jnp.concatenate(streams, axis=0)
```

```python
import functools

import jax
import jax.numpy as jnp
from jax import lax
from jax.experimental import pallas as pl
from jax.experimental.pallas import tpu as pltpu
from jax.experimental.pallas import tpu_sc as plsc

F32 = jnp.float32
BF16 = jnp.bfloat16
U32 = jnp.uint32
I32 = jnp.int32

D_MODEL = 1024
DEPTH = 4
D_CONV_A = 512
CONV_A_WIDTH = 3
D_CONV_B = 512
CONV_B_WIDTH = 31
N_HEADS = 16
HEAD_DIM = 64
D_ATTN = N_HEADS * HEAD_DIM
N_GROUPS = 4
EXPERTS_PER_GROUP = 8
N_EXPERTS = N_GROUPS * EXPERTS_PER_GROUP
TOP_K = 2
D_EXPERT = 512
LN_EPS = 1e-5
DEEPNORM_ALPHA = (2 * DEPTH) ** 0.25
D_IN_EVEN = 3 * D_CONV_A + 2 * D_CONV_B

LANES = 128
SUBLANES = 8
ROW_TILE = 512
COMBINE_TILE = 1024
CONV_HALO = 32
CONV_ROWS = 128
EVEN_SECTION = 128
ATTN_BLOCK = 256
ATTN_HEADS = 8
EXPERT_CHUNK = 512
SC_WINDOW = 64
COMBINE_PIECES = 2
ROUTER_COLS = LANES
VMEM_LIMIT = 56 * 1024 * 1024
NEG_LARGE = -1e30
SOFTPLUS_LINEAR = 80.0
DECAY_STOP = 104.0

ROUTE_ID, ROUTE_GATE, ROUTE_RANK = 0, TOP_K, 2 * TOP_K
ROUTE_ROWS = 8


def _layer_norm(v, g, b):
    mu = jnp.mean(v, axis=-1, keepdims=True)
    c = v - mu
    var = jnp.mean(c * c, axis=-1, keepdims=True)
    return c * lax.rsqrt(var + LN_EPS) * g + b


def _pack_bf16_pairs(v):
    half = v.shape[1] // 2
    hi = lax.bitcast_convert_type(v[:, :half].astype(BF16).astype(F32), U32)
    lo = lax.bitcast_convert_type(v[:, half:].astype(BF16).astype(F32), U32)
    return hi | (lo >> 16)


def _unpack_bf16_pairs(w):
    hi = lax.bitcast_convert_type(w & jnp.uint32(0xFFFF0000), F32)
    lo = lax.bitcast_convert_type(w << 16, F32)
    return jnp.concatenate([hi, lo], axis=1)


def _router_logits(h, rw_ref):
    hi = h.astype(BF16)
    lo = (h - hi.astype(F32)).astype(BF16)
    r1 = jnp.dot(hi, rw_ref[...], preferred_element_type=F32)
    r2 = jnp.dot(lo, rw_ref[:, :ROUTER_COLS], preferred_element_type=F32)
    return r1[:, :ROUTER_COLS] + r1[:, ROUTER_COLS:] + r2


def _route(logits, tri_ref, base):
    lane = lax.broadcasted_iota(I32, logits.shape, 1)
    lane_f = lane.astype(F32)

    def top(vals):
        m = jnp.max(vals, axis=1, keepdims=True)
        return m, jnp.min(jnp.where(vals == m, lane_f, float(LANES)), axis=1, keepdims=True)

    is_group = lane < N_GROUPS
    g_max, grp = top(jnp.where(is_group, logits, NEG_LARGE))
    g_gate = 1.0 / jnp.sum(jnp.where(is_group, jnp.exp(logits - g_max), 0.0), axis=1, keepdims=True)
    first = N_GROUPS + EXPERTS_PER_GROUP * grp
    e_logits = jnp.where((lane_f >= first) & (lane_f < first + EXPERTS_PER_GROUP), logits, NEG_LARGE)
    v0, l0 = top(e_logits)
    v1, l1 = top(jnp.where(lane_f == l0, NEG_LARGE, e_logits))
    e = jnp.exp(v1 - v0)
    gate0 = g_gate / (1.0 + e)
    gate1 = gate0 * e

    hot0 = jnp.where(lane_f == l0, 1.0, 0.0)
    hot1 = jnp.where(lane_f == l1, 1.0, 0.0)
    before0 = jnp.dot(tri_ref[...], hot0.astype(BF16), preferred_element_type=F32)
    before1 = jnp.dot(tri_ref[...], hot1.astype(BF16), preferred_element_type=F32)
    n0 = jnp.sum(hot0, axis=0, keepdims=True)
    n1 = jnp.sum(hot1, axis=0, keepdims=True)
    rank0 = jnp.sum(hot0 * (before0 + base), axis=1, keepdims=True)
    rank1 = jnp.sum(hot1 * (before1 + base + n0), axis=1, keepdims=True)

    fields = [l0 - N_GROUPS, l1 - N_GROUPS, gate0, gate1, rank0, rank1]
    record = jnp.zeros(logits.shape, F32)
    for i, f in enumerate(fields):
        record = jnp.where(lane == i, f, record)
    return record, base + n0 + n1


def _finish(x, mix, is_first_step, lng_ref, lnb_ref, rw_ref, rb_ref, tri_ref,
            h_ref, hp_ref, route_ref, route_t_ref, counts_ref, cnt_ref):
    h = _layer_norm(DEEPNORM_ALPHA * x + mix, lng_ref[...], lnb_ref[...])
    h_ref[0] = h
    hp_ref[0] = _pack_bf16_pairs(h)

    @pl.when(is_first_step)
    def _():
        cnt_ref[...] = jnp.zeros_like(cnt_ref)

    record, counts = _route(_router_logits(h, rw_ref) + rb_ref[...], tri_ref, cnt_ref[...])
    route_ref[0] = record
    route_t_ref[...] = record.T[:ROUTE_ROWS]
    cnt_ref[...] = counts
    counts_ref[...] = counts


def _const_spec(shape):
    return pl.BlockSpec(shape, lambda *_: (0,) * len(shape))


def _finish_specs(b, s, d, ts):
    in_specs = [_const_spec((1, d)), _const_spec((1, d)), _const_spec((d, 2 * ROUTER_COLS)),
                _const_spec((1, ROUTER_COLS)), _const_spec((ts, ts))]
    tile = lambda w: pl.BlockSpec((1, ts, w), lambda i, j: (i, j, 0))
    out_specs = [tile(d), tile(d // 2), tile(ROUTER_COLS),
                 pl.BlockSpec((ROUTE_ROWS, ts), lambda i, j: (0, i * (s // ts) + j)),
                 _const_spec((1, ROUTER_COLS))]
    out_shape = [jax.ShapeDtypeStruct((b, s, d), F32), jax.ShapeDtypeStruct((b, s, d // 2), U32),
                 jax.ShapeDtypeStruct((b, s, ROUTER_COLS), F32),
                 jax.ShapeDtypeStruct((ROUTE_ROWS, b * s), F32),
                 jax.ShapeDtypeStruct((1, ROUTER_COLS), F32)]
    return in_specs, out_specs, out_shape, [pltpu.VMEM((1, ROUTER_COLS), F32)]


def _finish_args(ln_g, ln_b, router, ts):
    rw, rb = router
    tri = (lax.broadcasted_iota(I32, (ts, ts), 1) < lax.broadcasted_iota(I32, (ts, ts), 0)).astype(BF16)
    return ln_g.reshape(1, -1), ln_b.reshape(1, -1), rw, rb, tri


def _is_first_step():
    return (pl.program_id(0) == 0) & (pl.program_id(1) == 0)


class _ShiftedWindow:
    def __init__(self, buf, start, back):
        self.back = back
        self.window = buf[start - back:start + CONV_ROWS, :]
        self.rolled = {0: self.window}

    def rows(self, d):
        first = self.back + d
        shift = first % SUBLANES
        if shift not in self.rolled:
            n, c = self.window.shape
            tiles = pltpu.roll(self.window.reshape(n // SUBLANES, SUBLANES, c), SUBLANES - shift, axis=1)
            sub = lax.broadcasted_iota(I32, (1, SUBLANES, 1), 1)
            self.rolled[shift] = jnp.where(sub < SUBLANES - shift, tiles[:-1], tiles[1:]).reshape(n - SUBLANES, c)
        return self.rolled[shift][first - shift:first - shift + CONV_ROWS]


def _even_kernel(x_ref, win_ref, ca_ref, cbw_ref, cbb_ref, nbg_ref, nbb_ref, wout_ref,
                 lng_ref, lnb_ref, rw_ref, rb_ref, tri_ref,
                 h_ref, hp_ref, route_ref, route_t_ref, counts_ref,
                 cnt_ref, u_buf, cv_buf, g_buf, y_buf):
    s = pl.program_id(1)
    ts = x_ref.shape[1]
    x = x_ref[0]

    @pl.when(s == 0)
    def _():
        cv_buf[0:CONV_HALO] = jnp.zeros((CONV_HALO, D_CONV_A), F32)
        g_buf[0:CONV_HALO] = jnp.zeros((CONV_HALO, D_CONV_B), F32)

    @pl.when(s > 0)
    def _():
        cv_buf[0:CONV_HALO] = cv_buf[ts:ts + CONV_HALO]
        g_buf[0:CONV_HALO] = g_buf[ts:ts + CONV_HALO]

    c0 = D_CONV_A
    sec = min(EVEN_SECTION, ts)

    def project(first_row):
        rows = slice(first_row, first_row + sec)
        halo_rows = slice(CONV_HALO + first_row, CONV_HALO + first_row + sec)
        u_buf[rows] = jnp.dot(x_ref[0, rows, :].astype(BF16), win_ref[...], preferred_element_type=F32)
        cv_buf[halo_rows] = u_buf[rows, c0:2 * c0] * u_buf[rows, 2 * c0:3 * c0]
        b_val = u_buf[rows, 3 * c0:3 * c0 + D_CONV_B]
        b_gate = u_buf[rows, 3 * c0 + D_CONV_B:]
        g_buf[halo_rows] = b_val * jax.nn.sigmoid(b_gate)

    project(0)
    for r0 in range(0, ts, CONV_ROWS):
        if r0 % sec == 0 and r0 + sec < ts:
            project(r0 + sec)
        win_a = _ShiftedWindow(cv_buf, CONV_HALO + r0, SUBLANES)
        acc_a = None
        for k in range(CONV_A_WIDTH):
            term = ca_ref[k:k + 1, :] * win_a.rows(k - (CONV_A_WIDTH - 1))
            acc_a = term if acc_a is None else acc_a + term
        y_a = u_buf[r0:r0 + CONV_ROWS, 0:c0] * acc_a
        win_b = _ShiftedWindow(g_buf, CONV_HALO + r0, CONV_HALO)
        acc_b = None
        for k in range(CONV_B_WIDTH):
            term = cbw_ref[k:k + 1, :] * win_b.rows(k - (CONV_B_WIDTH - 1))
            acc_b = term if acc_b is None else acc_b + term
        gb = _layer_norm(acc_b + cbb_ref[...], nbg_ref[...], nbb_ref[...])
        y_b = gb * jax.nn.sigmoid(gb)
        y_buf[r0:r0 + CONV_ROWS, 0:c0] = y_a.astype(BF16)
        y_buf[r0:r0 + CONV_ROWS, c0:] = y_b.astype(BF16)

    mix = jnp.dot(y_buf[...], wout_ref[...], preferred_element_type=F32)
    _finish(x, mix, _is_first_step(), lng_ref, lnb_ref, rw_ref, rb_ref, tri_ref,
            h_ref, hp_ref, route_ref, route_t_ref, counts_ref, cnt_ref)


def _even_mixer(x, w_in, conv_a, conv_b_w, conv_b_bias, norm_b_g, norm_b_b, w_out, ln_g, ln_b, router):
    b, s, d = x.shape
    ts = min(ROW_TILE, s)
    row = lambda v: v.reshape(1, -1)
    f_in, f_out, f_shape, f_scratch = _finish_specs(b, s, d, ts)
    return pl.pallas_call(
        _even_kernel,
        grid=(b, s // ts),
        in_specs=[
            pl.BlockSpec((1, ts, d), lambda i, j: (i, j, 0)),
            _const_spec((d, D_IN_EVEN)),
            _const_spec((CONV_A_WIDTH, D_CONV_A)),
            _const_spec((CONV_B_WIDTH, D_CONV_B)),
            _const_spec((1, D_CONV_B)),
            _const_spec((1, D_CONV_B)),
            _const_spec((1, D_CONV_B)),
            _const_spec((D_CONV_A + D_CONV_B, d)),
            *f_in,
        ],
        out_specs=f_out,
        out_shape=f_shape,
        scratch_shapes=[
            *f_scratch,
            pltpu.VMEM((ts, D_IN_EVEN), F32),
            pltpu.VMEM((CONV_HALO + ts, D_CONV_A), F32),
            pltpu.VMEM((CONV_HALO + ts, D_CONV_B), F32),
            pltpu.VMEM((ts, D_CONV_A + D_CONV_B), BF16),
        ],
        compiler_params=pltpu.CompilerParams(
            dimension_semantics=("arbitrary", "arbitrary"), vmem_limit_bytes=VMEM_LIMIT),
    )(x, w_in.astype(BF16), conv_a, conv_b_w, row(conv_b_bias), row(norm_b_g), row(norm_b_b),
      w_out.astype(BF16), *_finish_args(ln_g, ln_b, router, ts))


def _qkv_kernel(x_ref, w_ref, o_ref):
    u = jnp.dot(x_ref[0].astype(BF16), w_ref[...], preferred_element_type=F32)
    o_ref[0, :, :D_ATTN] = (u[:, :D_ATTN] * (HEAD_DIM ** -0.5)).astype(BF16)
    o_ref[0, :, D_ATTN:] = u[:, D_ATTN:].astype(BF16)


def _qkv_proj(x, w_qkv):
    b, s, d = x.shape
    ts = min(ROW_TILE, s)
    return pl.pallas_call(
        _qkv_kernel,
        grid=(b, s // ts),
        in_specs=[pl.BlockSpec((1, ts, d), lambda i, j: (i, j, 0)), _const_spec((d, 3 * D_ATTN))],
        out_specs=pl.BlockSpec((1, ts, 3 * D_ATTN), lambda i, j: (i, j, 0)),
        out_shape=jax.ShapeDtypeStruct((b, s, 3 * D_ATTN), BF16),
        compiler_params=pltpu.CompilerParams(
            dimension_semantics=("arbitrary", "arbitrary"), vmem_limit_bytes=VMEM_LIMIT),
    )(x, w_qkv.astype(BF16))


def _attn_kernel(q_ref, k_ref, v_ref, o_ref, acc_ref, carry_ref, z_buf, sp_buf, *, blk):
    s_len = q_ref.shape[1]
    per_tile = LANES // HEAD_DIM
    lane = lax.broadcasted_iota(jnp.int32, (1, LANES), 1)
    tri = (lax.broadcasted_iota(jnp.int32, (blk, blk), 0)
           >= lax.broadcasted_iota(jnp.int32, (blk, blk), 1)).astype(BF16)
    causal = (lax.broadcasted_iota(jnp.int32, (blk, blk), 1)
              < lax.broadcasted_iota(jnp.int32, (blk, blk), 0))

    def tile_of(h):
        return slice((h // per_tile) * LANES, (h // per_tile + 1) * LANES)

    def q_block(n_back, _):
        q0 = pl.multiple_of(n_back * blk, blk)
        qms = [jnp.where((lane // HEAD_DIM) == h % per_tile, q_ref[0, pl.ds(q0, blk), tile_of(h)],
                         jnp.zeros((), BF16)) for h in range(ATTN_HEADS)]
        acc_ref[...] = jnp.zeros_like(acc_ref)
        carry_ref[...] = jnp.zeros_like(carry_ref)

        def score(m, slot, diagonal):
            k0 = pl.multiple_of(q0 - m * blk, blk)
            for h in range(ATTN_HEADS):
                kb = k_ref[0, pl.ds(k0, blk), tile_of(h)]
                z = lax.dot_general(qms[h], kb, (((1,), (1,)), ((), ())),
                                    preferred_element_type=F32)
                sp = jnp.where(z > SOFTPLUS_LINEAR, z, jnp.log(1.0 + jnp.exp(z)))
                if diagonal:
                    sp = jnp.where(causal, sp, 0.0)
                z_buf[slot, h] = z
                sp_buf[slot, h] = sp.astype(BF16)

        def apply(m, slot, diagonal):
            k0 = pl.multiple_of(q0 - m * blk, blk)
            for h in range(ATTN_HEADS):
                vb = v_ref[0, pl.ds(k0, blk), tile_of(h)]
                c = jnp.dot(sp_buf[slot, h], tri, preferred_element_type=F32)
                carry = carry_ref[h]
                a = jnp.exp(z_buf[slot, h] - c - jnp.concatenate([carry] * (blk // LANES), axis=1))
                if diagonal:
                    a = jnp.where(causal, a, 0.0)
                acc_ref[h] += jnp.dot(a.astype(BF16), vb, preferred_element_type=F32)
                carry_ref[h] = carry + jnp.broadcast_to(c[:, 0:1], carry.shape)

        @pl.when(n_back == 0)
        def _():
            score(0, 0, True)
            apply(0, 0, True)

        def live():
            return (jnp.min(carry_ref[...]) <= DECAY_STOP).astype(jnp.int32)

        @pl.when(n_back > 0)
        def _():
            score(0, 0, True)
            score(1, 1, False)
            apply(0, 0, True)
            apply(1, 1, False)

            @pl.when((n_back >= 2) & (live() == 1))
            def _():
                score(2, 0, False)

                def pair(state):
                    m, _ = state
                    score(m + 1, 1, False)
                    apply(m, 0, False)

                    @pl.when((live() == 1) & (m + 2 <= n_back))
                    def _():
                        score(m + 2, 0, False)
                        apply(m + 1, 1, False)

                    return m + 2, live()

                _, alive = lax.while_loop(lambda st: (st[0] + 1 <= n_back) & (st[1] == 1), pair,
                                          (jnp.int32(2), jnp.int32(1)))

                @pl.when((alive == 1) & (n_back % 2 == 0))
                def _():
                    apply(n_back, 0, False)

                @pl.when((alive == 1) & (n_back % 2 == 1))
                def _():
                    apply(n_back, 1, False)

        for t in range(ATTN_HEADS // per_tile):
            o = acc_ref[t * per_tile]
            for h in range(1, per_tile):
                o = jnp.where((lane // HEAD_DIM) == h, acc_ref[t * per_tile + h], o)
            o_ref[0, pl.ds(q0, blk), t * LANES:(t + 1) * LANES] = o.astype(BF16)
        return 0

    lax.fori_loop(0, s_len // blk, q_block, 0)


def _attention(qkv):
    b, s, _ = qkv.shape
    blk = min(ATTN_BLOCK, s)
    width = ATTN_HEADS * HEAD_DIM
    n_steps = D_ATTN // width
    return pl.pallas_call(
        functools.partial(_attn_kernel, blk=blk),
        grid=(b, n_steps),
        in_specs=[
            pl.BlockSpec((1, s, width), lambda i, p: (i, 0, p)),
            pl.BlockSpec((1, s, width), lambda i, p: (i, 0, n_steps + p)),
            pl.BlockSpec((1, s, width), lambda i, p: (i, 0, 2 * n_steps + p)),
        ],
        out_specs=pl.BlockSpec((1, s, width), lambda i, p: (i, 0, p)),
        out_shape=jax.ShapeDtypeStruct((b, s, D_ATTN), BF16),
        scratch_shapes=[pltpu.VMEM((ATTN_HEADS, blk, LANES), F32),
                        pltpu.VMEM((ATTN_HEADS, blk, LANES), F32),
                        pltpu.VMEM((2, ATTN_HEADS, blk, blk), F32),
                        pltpu.VMEM((2, ATTN_HEADS, blk, blk), BF16)],
        compiler_params=pltpu.CompilerParams(
            dimension_semantics=("arbitrary", "arbitrary"), vmem_limit_bytes=VMEM_LIMIT),
    )(qkv, qkv, qkv)


def _oproj_kernel(o_ref, x_ref, w_ref, lng_ref, lnb_ref, rw_ref, rb_ref, tri_ref,
                  h_ref, hp_ref, route_ref, route_t_ref, counts_ref, cnt_ref):
    mix = jnp.dot(o_ref[0], w_ref[...], preferred_element_type=F32)
    _finish(x_ref[0], mix, _is_first_step(), lng_ref, lnb_ref, rw_ref, rb_ref, tri_ref,
            h_ref, hp_ref, route_ref, route_t_ref, counts_ref, cnt_ref)


def _attn_out(o, x, w_o, ln_g, ln_b, router):
    b, s, d = x.shape
    ts = min(ROW_TILE, s)
    f_in, f_out, f_shape, f_scratch = _finish_specs(b, s, d, ts)
    return pl.pallas_call(
        _oproj_kernel,
        grid=(b, s // ts),
        in_specs=[
            pl.BlockSpec((1, ts, D_ATTN), lambda i, j: (i, j, 0)),
            pl.BlockSpec((1, ts, d), lambda i, j: (i, j, 0)),
            _const_spec((D_ATTN, d)),
            *f_in,
        ],
        out_specs=f_out,
        out_shape=f_shape,
        scratch_shapes=f_scratch,
        compiler_params=pltpu.CompilerParams(
            dimension_semantics=("arbitrary", "arbitrary"), vmem_limit_bytes=VMEM_LIMIT),
    )(o, x, w_o.astype(BF16), *_finish_args(ln_g, ln_b, router, ts))


def _dispatch_meta(route_t, counts, chunk):
    t = route_t.shape[1]
    sizes = counts[0, N_GROUPS:N_GROUPS + N_EXPERTS].astype(I32)
    ids = route_t[ROUTE_ID:ROUTE_ID + TOP_K].astype(I32)
    rank = route_t[ROUTE_RANK:ROUTE_RANK + TOP_K].astype(I32)
    padded = (sizes + chunk - 1) // chunk * chunk
    pad_end = jnp.cumsum(padded)
    pad_start = pad_end - padded
    experts = jnp.arange(N_EXPERTS, dtype=I32)
    seg_start = jnp.sum(jnp.where(ids[None] == experts[:, None, None], pad_start[:, None, None], 0), axis=0)
    dest = (seg_start + rank).astype(I32)
    n_chunks = -(-(t * TOP_K) // chunk) + N_EXPERTS
    chunk_first_row = jnp.arange(n_chunks, dtype=I32) * chunk
    chunk_expert = jnp.minimum(
        jnp.sum((chunk_first_row[:, None] >= pad_end[None, :]).astype(I32), axis=1), N_EXPERTS - 1)
    chunk_rows = jnp.clip(
        pad_start[chunk_expert] + sizes[chunk_expert] - chunk_first_row, 0, chunk).astype(I32)
    n_valid = (pad_end[-1] // chunk).astype(I32).reshape(1)
    return dest, chunk_expert, chunk_rows, n_valid, n_chunks


def _index_windows(idx):
    return jnp.pad(idx.reshape(-1, SC_WINDOW), ((0, 0), (0, LANES - SC_WINDOW)))


def _sc_row_scatter(x, dest, n_out):
    t, d = x.shape
    mesh = plsc.VectorSubcoreMesh(core_axis_name="core", subcore_axis_name="subcore")
    idx = [_index_windows(dest[k]) for k in range(TOP_K)]

    @pl.kernel(out_type=jax.ShapeDtypeStruct((n_out, d), x.dtype), mesh=mesh, scratch_types=[])
    def scatter(x_hbm, *rest):
        i_hbm, o_hbm = rest[:TOP_K], rest[TOP_K]

        def body(x_vmem, *i_vmem):
            for k in range(TOP_K):
                pltpu.sync_copy(x_vmem, o_hbm.at[i_vmem[k].at[0, pl.ds(0, SC_WINDOW)]])

        pltpu.emit_pipeline(
            body,
            grid=(t // SC_WINDOW,),
            in_specs=[pl.BlockSpec((SC_WINDOW, d), lambda i: (i, 0))]
            + [pl.BlockSpec((1, LANES), lambda i: (i, 0))] * TOP_K,
            out_specs=[],
            core_axis_name=("core", "subcore"),
            dimension_semantics=(pltpu.PARALLEL,),
        )(x_hbm, *i_hbm)

    return scatter(x, *idx)


def _sc_row_gather(x, idx):
    _, d = x.shape
    m = idx.shape[0]
    mesh = plsc.VectorSubcoreMesh(core_axis_name="core", subcore_axis_name="subcore")

    @pl.kernel(out_type=jax.ShapeDtypeStruct((m, d), x.dtype), mesh=mesh, scratch_types=[])
    def gather(x_hbm, i_hbm, o_hbm):
        def body(i_vmem, o_vmem):
            pltpu.sync_copy(x_hbm.at[i_vmem.at[0, pl.ds(0, SC_WINDOW)]], o_vmem)

        pltpu.emit_pipeline(
            body,
            grid=(m // SC_WINDOW,),
            in_specs=[pl.BlockSpec((1, LANES), lambda i: (i, 0))],
            out_specs=[pl.BlockSpec((SC_WINDOW, d), lambda i: (i, 0))],
            core_axis_name=("core", "subcore"),
            dimension_semantics=(pltpu.PARALLEL,),
        )(i_hbm, o_hbm)

    return gather(x, _index_windows(idx))


def _expert_kernel(ce_ref, rows_ref, nv_ref, x_ref, wup_ref, wdn_ref, y_ref, wup_bf, wdn_bf):
    i = pl.program_id(0)

    @pl.when(i < nv_ref[0])
    def _():
        prev = ce_ref[jnp.maximum(i - 1, 0)]

        @pl.when((i == 0) | (ce_ref[i] != prev))
        def _():
            wup_bf[...] = wup_ref[0, 0].astype(BF16)
            wdn_bf[...] = wdn_ref[0, 0].astype(BF16)

        row = lax.broadcasted_iota(I32, x_ref.shape, 0)
        packed = jnp.where(row < rows_ref[i], x_ref[...], jnp.uint32(0))
        gu = jnp.dot(_unpack_bf16_pairs(packed).astype(BF16), wup_bf[...], preferred_element_type=F32)
        gate, up = gu[:, :D_EXPERT], gu[:, D_EXPERT:]
        act = gate * jax.nn.sigmoid(gate) * up
        y = jnp.dot(act.astype(BF16), wdn_bf[...], preferred_element_type=F32)
        y_ref[...] = _pack_bf16_pairs(y)

    @pl.when(i >= nv_ref[0])
    def _():
        y_ref[...] = jnp.zeros_like(y_ref)


def _experts(x_rows, chunk_expert, chunk_rows, n_valid, n_chunks, chunk, layer, w_up, w_down):
    _, half = x_rows.shape
    d = 2 * half
    grid_spec = pltpu.PrefetchScalarGridSpec(
        num_scalar_prefetch=3,
        grid=(n_chunks,),
        in_specs=[
            pl.BlockSpec((chunk, half), lambda i, ce, rows, nv: (jnp.minimum(i, nv[0] - 1), 0)),
            pl.BlockSpec((1, 1, d, 2 * D_EXPERT), lambda i, ce, rows, nv: (layer, ce[i], 0, 0)),
            pl.BlockSpec((1, 1, D_EXPERT, d), lambda i, ce, rows, nv: (layer, ce[i], 0, 0)),
        ],
        out_specs=pl.BlockSpec((chunk, half), lambda i, ce, rows, nv: (i, 0)),
        scratch_shapes=[
            pltpu.VMEM((d, 2 * D_EXPERT), BF16),
            pltpu.VMEM((D_EXPERT, d), BF16),
        ],
    )
    return pl.pallas_call(
        _expert_kernel,
        grid_spec=grid_spec,
        out_shape=jax.ShapeDtypeStruct((n_chunks * chunk, half), U32),
        compiler_params=pltpu.CompilerParams(
            dimension_semantics=("arbitrary",), vmem_limit_bytes=VMEM_LIMIT),
    )(chunk_expert, chunk_rows, n_valid, x_rows, w_up, w_down)


def _combine_kernel(h_ref, route_ref, *rest):
    y_refs, (lng_ref, lnb_ref), o_ref = rest[:TOP_K], rest[TOP_K:TOP_K + 2], rest[-1]
    gates = route_ref[:, ROUTE_GATE:ROUTE_GATE + TOP_K]
    ffn = gates[:, 0:1] * _unpack_bf16_pairs(y_refs[0][...])
    for k in range(1, TOP_K):
        ffn = ffn + gates[:, k:k + 1] * _unpack_bf16_pairs(y_refs[k][...])
    o_ref[...] = _layer_norm(DEEPNORM_ALPHA * h_ref[...] + ffn, lng_ref[...], lnb_ref[...])


def _combine_piece(h2d, route, y_piece, ln_g, ln_b, piece, out_so_far):
    t, d = h2d.shape
    rows = t // COMBINE_PIECES
    tm = min(COMBINE_TILE, rows)
    tiles = rows // tm
    row = lambda v: v.reshape(1, -1)
    here = lambda i: (piece * tiles + i, 0)
    y_specs = [pl.BlockSpec((tm, d // 2), functools.partial(lambda i, k: (k * tiles + i, 0), k=k))
               for k in range(TOP_K)]
    operands = [h2d, route, *([y_piece] * TOP_K), row(ln_g), row(ln_b)]
    in_specs = [pl.BlockSpec((tm, d), here), pl.BlockSpec((tm, ROUTER_COLS), here),
                *y_specs, _const_spec((1, d)), _const_spec((1, d))]
    aliases = {}
    if out_so_far is not None:
        aliases = {len(operands): 0}
        operands.append(out_so_far)
        in_specs.append(pl.BlockSpec(memory_space=pl.ANY))
    return pl.pallas_call(
        _combine_kernel,
        grid=(tiles,),
        in_specs=in_specs,
        out_specs=pl.BlockSpec((tm, d), here),
        out_shape=jax.ShapeDtypeStruct((t, d), F32),
        input_output_aliases=aliases,
        compiler_params=pltpu.CompilerParams(
            dimension_semantics=("arbitrary",), vmem_limit_bytes=VMEM_LIMIT),
    )(*operands)


def _moe_block(h, h_packed, route, route_t, counts, layer, w_up, w_down, ln_g, ln_b):
    b, s, d = h.shape
    t = b * s
    chunk = min(EXPERT_CHUNK, t)
    dest, chunk_expert, chunk_rows, n_valid, n_chunks = _dispatch_meta(route_t, counts, chunk)
    x_rows = _sc_row_scatter(h_packed.reshape(t, d // 2), dest, n_chunks * chunk)
    y_rows = _experts(x_rows, chunk_expert, chunk_rows, n_valid, n_chunks, chunk, layer, w_up, w_down)
    rows = t // COMBINE_PIECES
    out = None
    for piece in range(COMBINE_PIECES):
        y_piece = _sc_row_gather(y_rows, dest[:, piece * rows:(piece + 1) * rows].reshape(-1))
        out = _combine_piece(h.reshape(t, d), route.reshape(t, ROUTER_COLS), y_piece, ln_g, ln_b,
                             piece, out)
    return out.reshape(b, s, d)


def _router_params(rg_w, rg_b, re_w, re_b):
    w = jnp.concatenate([rg_w, re_w], axis=1)
    w = jnp.pad(w, ((0, 0), (0, ROUTER_COLS - w.shape[1])))
    hi = w.astype(BF16)
    lo = (w - hi.astype(F32)).astype(BF16)
    bias = jnp.concatenate([rg_b, re_b])
    bias = jnp.pad(bias, (0, ROUTER_COLS - bias.shape[0])).reshape(1, ROUTER_COLS)
    return jnp.concatenate([hi, lo], axis=1), bias


def kernel(x, even_w_in, even_conv_a, even_conv_b_w, even_conv_b_bias, even_norm_b_g,
           even_norm_b_b, even_w_out, odd_w_qkv, odd_w_o, ln_mix_g, ln_mix_b, ln_ffn_g,
           ln_ffn_b, router_group_w, router_group_b, router_expert_w, router_expert_b,
           expert_w_up, expert_w_down):
    for layer in range(DEPTH):
        i = layer // 2
        router = _router_params(router_group_w[layer], router_group_b[layer],
                                router_expert_w[layer], router_expert_b[layer])
        if layer % 2 == 0:
            mixed = _even_mixer(
                x, even_w_in[i], even_conv_a[i], even_conv_b_w[i], even_conv_b_bias[i],
                even_norm_b_g[i], even_norm_b_b[i], even_w_out[i], ln_mix_g[layer], ln_mix_b[layer],
                router)
        else:
            mixed = _attn_out(_attention(_qkv_proj(x, odd_w_qkv[i])), x, odd_w_o[i],
                              ln_mix_g[layer], ln_mix_b[layer], router)
        x = _moe_block(*mixed, layer, expert_w_up, expert_w_down, ln_ffn_g[layer], ln_ffn_b[layer])
    return x
```

```python
import functools

import jax
import jax.numpy as jnp
from jax import lax
from jax.experimental import pallas as pl
from jax.experimental.pallas import tpu as pltpu
from jax.experimental.pallas import tpu_sc as plsc

F32 = jnp.float32
BF16 = jnp.bfloat16
U32 = jnp.uint32
I32 = jnp.int32

D_MODEL = 1024
DEPTH = 4
D_CONV_A = 512
CONV_A_WIDTH = 3
D_CONV_B = 512
CONV_B_WIDTH = 31
N_HEADS = 16
HEAD_DIM = 64
D_ATTN = N_HEADS * HEAD_DIM
N_GROUPS = 4
EXPERTS_PER_GROUP = 8
N_EXPERTS = N_GROUPS * EXPERTS_PER_GROUP
TOP_K = 2
D_EXPERT = 512
LN_EPS = 1e-5
DEEPNORM_ALPHA = (2 * DEPTH) ** 0.25
D_IN_EVEN = 3 * D_CONV_A + 2 * D_CONV_B

LANES = 128
SUBLANES = 8
ROW_TILE = 512
COMBINE_TILE = 1024
CONV_HALO = 32
CONV_ROWS = 128
EVEN_SECTION = 128
ATTN_BLOCK = 256
ATTN_HEADS = 8
EXPERT_CHUNK = 1024
SC_WINDOW = 64
COMBINE_PIECES = 2
ROUTER_COLS = LANES
VMEM_LIMIT = 56 * 1024 * 1024
NEG_LARGE = -1e30
SOFTPLUS_LINEAR = 80.0
DECAY_STOP = 104.0

ROUTE_ID, ROUTE_GATE, ROUTE_RANK = 0, TOP_K, 2 * TOP_K
ROUTE_ROWS = 8


def _layer_norm(v, g, b):
    mu = jnp.mean(v, axis=-1, keepdims=True)
    c = v - mu
    var = jnp.mean(c * c, axis=-1, keepdims=True)
    return c * lax.rsqrt(var + LN_EPS) * g + b


def _pack_bf16_pairs(v):
    half = v.shape[1] // 2
    hi = lax.bitcast_convert_type(v[:, :half].astype(BF16).astype(F32), U32)
    lo = lax.bitcast_convert_type(v[:, half:].astype(BF16).astype(F32), U32)
    return hi | (lo >> 16)


def _unpack_bf16_pairs(w):
    hi = lax.bitcast_convert_type(w & jnp.uint32(0xFFFF0000), F32)
    lo = lax.bitcast_convert_type(w << 16, F32)
    return jnp.concatenate([hi, lo], axis=1)


def _router_logits(h, rw_ref):
    hi = h.astype(BF16)
    lo = (h - hi.astype(F32)).astype(BF16)
    r1 = jnp.dot(hi, rw_ref[...], preferred_element_type=F32)
    r2 = jnp.dot(lo, rw_ref[:, :ROUTER_COLS], preferred_element_type=F32)
    return r1[:, :ROUTER_COLS] + r1[:, ROUTER_COLS:] + r2


def _route(logits, tri_ref, base):
    lane = lax.broadcasted_iota(I32, logits.shape, 1)
    lane_f = lane.astype(F32)

    def top(vals):
        m = jnp.max(vals, axis=1, keepdims=True)
        return m, jnp.min(jnp.where(vals == m, lane_f, float(LANES)), axis=1, keepdims=True)

    is_group = lane < N_GROUPS
    g_max, grp = top(jnp.where(is_group, logits, NEG_LARGE))
    g_gate = 1.0 / jnp.sum(jnp.where(is_group, jnp.exp(logits - g_max), 0.0), axis=1, keepdims=True)
    first = N_GROUPS + EXPERTS_PER_GROUP * grp
    e_logits = jnp.where((lane_f >= first) & (lane_f < first + EXPERTS_PER_GROUP), logits, NEG_LARGE)
    v0, l0 = top(e_logits)
    v1, l1 = top(jnp.where(lane_f == l0, NEG_LARGE, e_logits))
    e = jnp.exp(v1 - v0)
    gate0 = g_gate / (1.0 + e)
    gate1 = gate0 * e

    hot0 = jnp.where(lane_f == l0, 1.0, 0.0)
    hot1 = jnp.where(lane_f == l1, 1.0, 0.0)
    before0 = jnp.dot(tri_ref[...], hot0.astype(BF16), preferred_element_type=F32)
    before1 = jnp.dot(tri_ref[...], hot1.astype(BF16), preferred_element_type=F32)
    n0 = jnp.sum(hot0, axis=0, keepdims=True)
    n1 = jnp.sum(hot1, axis=0, keepdims=True)
    rank0 = jnp.sum(hot0 * (before0 + base), axis=1, keepdims=True)
    rank1 = jnp.sum(hot1 * (before1 + base + n0), axis=1, keepdims=True)

    fields = [l0 - N_GROUPS, l1 - N_GROUPS, gate0, gate1, rank0, rank1]
    record = jnp.zeros(logits.shape, F32)
    for i, f in enumerate(fields):
        record = jnp.where(lane == i, f, record)
    return record, base + n0 + n1


def _finish(x, mix, is_first_step, lng_ref, lnb_ref, rw_ref, rb_ref, tri_ref,
            h_ref, hp_ref, route_ref, route_t_ref, counts_ref, cnt_ref):
    h = _layer_norm(DEEPNORM_ALPHA * x + mix, lng_ref[...], lnb_ref[...])
    h_ref[0] = h
    hp_ref[0] = _pack_bf16_pairs(h)

    @pl.when(is_first_step)
    def _():
        cnt_ref[...] = jnp.zeros_like(cnt_ref)

    record, counts = _route(_router_logits(h, rw_ref) + rb_ref[...], tri_ref, cnt_ref[...])
    route_ref[0] = record
    route_t_ref[...] = record.T[:ROUTE_ROWS]
    cnt_ref[...] = counts
    counts_ref[...] = counts


def _const_spec(shape):
    return pl.BlockSpec(shape, lambda *_: (0,) * len(shape))


def _finish_specs(b, s, d, ts):
    in_specs = [_const_spec((1, d)), _const_spec((1, d)), _const_spec((d, 2 * ROUTER_COLS)),
                _const_spec((1, ROUTER_COLS)), _const_spec((ts, ts))]
    tile = lambda w: pl.BlockSpec((1, ts, w), lambda i, j: (i, j, 0))
    out_specs = [tile(d), tile(d // 2), tile(ROUTER_COLS),
                 pl.BlockSpec((ROUTE_ROWS, ts), lambda i, j: (0, i * (s // ts) + j)),
                 _const_spec((1, ROUTER_COLS))]
    out_shape = [jax.ShapeDtypeStruct((b, s, d), F32), jax.ShapeDtypeStruct((b, s, d // 2), U32),
                 jax.ShapeDtypeStruct((b, s, ROUTER_COLS), F32),
                 jax.ShapeDtypeStruct((ROUTE_ROWS, b * s), F32),
                 jax.ShapeDtypeStruct((1, ROUTER_COLS), F32)]
    return in_specs, out_specs, out_shape, [pltpu.VMEM((1, ROUTER_COLS), F32)]


def _finish_args(ln_g, ln_b, router, ts):
    rw, rb = router
    tri = (lax.broadcasted_iota(I32, (ts, ts), 1) < lax.broadcasted_iota(I32, (ts, ts), 0)).astype(BF16)
    return ln_g.reshape(1, -1), ln_b.reshape(1, -1), rw, rb, tri


def _is_first_step():
    return (pl.program_id(0) == 0) & (pl.program_id(1) == 0)


class _ShiftedWindow:
    def __init__(self, buf, start, back):
        self.back = back
        self.window = buf[start - back:start + CONV_ROWS, :]
        self.rolled = {0: self.window}

    def rows(self, d):
        first = self.back + d
        shift = first % SUBLANES
        if shift not in self.rolled:
            n, c = self.window.shape
            tiles = pltpu.roll(self.window.reshape(n // SUBLANES, SUBLANES, c), SUBLANES - shift, axis=1)
            sub = lax.broadcasted_iota(I32, (1, SUBLANES, 1), 1)
            self.rolled[shift] = jnp.where(sub < SUBLANES - shift, tiles[:-1], tiles[1:]).reshape(n - SUBLANES, c)
        return self.rolled[shift][first - shift:first - shift + CONV_ROWS]


def _even_kernel(x_ref, win_ref, ca_ref, cbw_ref, cbb_ref, nbg_ref, nbb_ref, wout_ref,
                 lng_ref, lnb_ref, rw_ref, rb_ref, tri_ref,
                 h_ref, hp_ref, route_ref, route_t_ref, counts_ref,
                 cnt_ref, u_buf, cv_buf, g_buf, y_buf):
    s = pl.program_id(1)
    ts = x_ref.shape[1]
    x = x_ref[0]

    @pl.when(s == 0)
    def _():
        cv_buf[0:CONV_HALO] = jnp.zeros((CONV_HALO, D_CONV_A), F32)
        g_buf[0:CONV_HALO] = jnp.zeros((CONV_HALO, D_CONV_B), F32)

    @pl.when(s > 0)
    def _():
        cv_buf[0:CONV_HALO] = cv_buf[ts:ts + CONV_HALO]
        g_buf[0:CONV_HALO] = g_buf[ts:ts + CONV_HALO]

    c0 = D_CONV_A
    sec = min(EVEN_SECTION, ts)

    def project(first_row):
        rows = slice(first_row, first_row + sec)
        halo_rows = slice(CONV_HALO + first_row, CONV_HALO + first_row + sec)
        u_buf[rows] = jnp.dot(x_ref[0, rows, :].astype(BF16), win_ref[...], preferred_element_type=F32)
        cv_buf[halo_rows] = u_buf[rows, c0:2 * c0] * u_buf[rows, 2 * c0:3 * c0]
        b_val = u_buf[rows, 3 * c0:3 * c0 + D_CONV_B]
        b_gate = u_buf[rows, 3 * c0 + D_CONV_B:]
        g_buf[halo_rows] = b_val * jax.nn.sigmoid(b_gate)

    project(0)
    for r0 in range(0, ts, CONV_ROWS):
        if r0 % sec == 0 and r0 + sec < ts:
            project(r0 + sec)
        win_a = _ShiftedWindow(cv_buf, CONV_HALO + r0, SUBLANES)
        acc_a = None
        for k in range(CONV_A_WIDTH):
            term = ca_ref[k:k + 1, :] * win_a.rows(k - (CONV_A_WIDTH - 1))
            acc_a = term if acc_a is None else acc_a + term
        y_a = u_buf[r0:r0 + CONV_ROWS, 0:c0] * acc_a
        win_b = _ShiftedWindow(g_buf, CONV_HALO + r0, CONV_HALO)
        acc_b = None
        for k in range(CONV_B_WIDTH):
            term = cbw_ref[k:k + 1, :] * win_b.rows(k - (CONV_B_WIDTH - 1))
            acc_b = term if acc_b is None else acc_b + term
        gb = _layer_norm(acc_b + cbb_ref[...], nbg_ref[...], nbb_ref[...])
        y_b = gb * jax.nn.sigmoid(gb)
        y_buf[r0:r0 + CONV_ROWS, 0:c0] = y_a.astype(BF16)
        y_buf[r0:r0 + CONV_ROWS, c0:] = y_b.astype(BF16)

    mix = jnp.dot(y_buf[...], wout_ref[...], preferred_element_type=F32)
    _finish(x, mix, _is_first_step(), lng_ref, lnb_ref, rw_ref, rb_ref, tri_ref,
            h_ref, hp_ref, route_ref, route_t_ref, counts_ref, cnt_ref)


def _even_mixer(x, w_in, conv_a, conv_b_w, conv_b_bias, norm_b_g, norm_b_b, w_out, ln_g, ln_b, router):
    b, s, d = x.shape
    ts = min(ROW_TILE, s)
    row = lambda v: v.reshape(1, -1)
    f_in, f_out, f_shape, f_scratch = _finish_specs(b, s, d, ts)
    return pl.pallas_call(
        _even_kernel,
        grid=(b, s // ts),
        in_specs=[
            pl.BlockSpec((1, ts, d), lambda i, j: (i, j, 0)),
            _const_spec((d, D_IN_EVEN)),
            _const_spec((CONV_A_WIDTH, D_CONV_A)),
            _const_spec((CONV_B_WIDTH, D_CONV_B)),
            _const_spec((1, D_CONV_B)),
            _const_spec((1, D_CONV_B)),
            _const_spec((1, D_CONV_B)),
            _const_spec((D_CONV_A + D_CONV_B, d)),
            *f_in,
        ],
        out_specs=f_out,
        out_shape=f_shape,
        scratch_shapes=[
            *f_scratch,
            pltpu.VMEM((ts, D_IN_EVEN), F32),
            pltpu.VMEM((CONV_HALO + ts, D_CONV_A), F32),
            pltpu.VMEM((CONV_HALO + ts, D_CONV_B), F32),
            pltpu.VMEM((ts, D_CONV_A + D_CONV_B), BF16),
        ],
        compiler_params=pltpu.CompilerParams(
            dimension_semantics=("arbitrary", "arbitrary"), vmem_limit_bytes=VMEM_LIMIT),
    )(x, w_in.astype(BF16), conv_a, conv_b_w, row(conv_b_bias), row(norm_b_g), row(norm_b_b),
      w_out.astype(BF16), *_finish_args(ln_g, ln_b, router, ts))


def _qkv_kernel(x_ref, w_ref, o_ref):
    u = jnp.dot(x_ref[0].astype(BF16), w_ref[...], preferred_element_type=F32)
    o_ref[0, :, :D_ATTN] = (u[:, :D_ATTN] * (HEAD_DIM ** -0.5)).astype(BF16)
    o_ref[0, :, D_ATTN:] = u[:, D_ATTN:].astype(BF16)


def _qkv_proj(x, w_qkv):
    b, s, d = x.shape
    ts = min(ROW_TILE, s)
    return pl.pallas_call(
        _qkv_kernel,
        grid=(b, s // ts),
        in_specs=[pl.BlockSpec((1, ts, d), lambda i, j: (i, j, 0)), _const_spec((d, 3 * D_ATTN))],
        out_specs=pl.BlockSpec((1, ts, 3 * D_ATTN), lambda i, j: (i, j, 0)),
        out_shape=jax.ShapeDtypeStruct((b, s, 3 * D_ATTN), BF16),
        compiler_params=pltpu.CompilerParams(
            dimension_semantics=("arbitrary", "arbitrary"), vmem_limit_bytes=VMEM_LIMIT),
    )(x, w_qkv.astype(BF16))


def _attn_kernel(q_ref, k_ref, v_ref, o_ref, acc_ref, carry_ref, z_buf, sp_buf, *, blk):
    s_len = q_ref.shape[1]
    per_tile = LANES // HEAD_DIM
    lane = lax.broadcasted_iota(jnp.int32, (1, LANES), 1)
    tri = (lax.broadcasted_iota(jnp.int32, (blk, blk), 0)
           >= lax.broadcasted_iota(jnp.int32, (blk, blk), 1)).astype(BF16)
    causal = (lax.broadcasted_iota(jnp.int32, (blk, blk), 1)
              < lax.broadcasted_iota(jnp.int32, (blk, blk), 0))

    def tile_of(h):
        return slice((h // per_tile) * LANES, (h // per_tile + 1) * LANES)

    def q_block(n_back, _):
        q0 = pl.multiple_of(n_back * blk, blk)
        qms = [jnp.where((lane // HEAD_DIM) == h % per_tile, q_ref[0, pl.ds(q0, blk), tile_of(h)],
                         jnp.zeros((), BF16)) for h in range(ATTN_HEADS)]
        acc_ref[...] = jnp.zeros_like(acc_ref)
        carry_ref[...] = jnp.zeros_like(carry_ref)

        def score(m, slot, diagonal):
            k0 = pl.multiple_of(q0 - m * blk, blk)
            for h in range(ATTN_HEADS):
                kb = k_ref[0, pl.ds(k0, blk), tile_of(h)]
                z = lax.dot_general(qms[h], kb, (((1,), (1,)), ((), ())),
                                    preferred_element_type=F32)
                sp = jnp.where(z > SOFTPLUS_LINEAR, z, jnp.log(1.0 + jnp.exp(z)))
                if diagonal:
                    sp = jnp.where(causal, sp, 0.0)
                z_buf[slot, h] = z
                sp_buf[slot, h] = sp.astype(BF16)

        def apply(m, slot, diagonal):
            k0 = pl.multiple_of(q0 - m * blk, blk)
            for h in range(ATTN_HEADS):
                vb = v_ref[0, pl.ds(k0, blk), tile_of(h)]
                c = jnp.dot(sp_buf[slot, h], tri, preferred_element_type=F32)
                carry = carry_ref[h]
                a = jnp.exp(z_buf[slot, h] - c - jnp.concatenate([carry] * (blk // LANES), axis=1))
                if diagonal:
                    a = jnp.where(causal, a, 0.0)
                acc_ref[h] += jnp.dot(a.astype(BF16), vb, preferred_element_type=F32)
                carry_ref[h] = carry + jnp.broadcast_to(c[:, 0:1], carry.shape)

        @pl.when(n_back == 0)
        def _():
            score(0, 0, True)
            apply(0, 0, True)

        def live():
            return (jnp.min(carry_ref[...]) <= DECAY_STOP).astype(jnp.int32)

        @pl.when(n_back > 0)
        def _():
            score(0, 0, True)
            score(1, 1, False)
            apply(0, 0, True)
            apply(1, 1, False)

            @pl.when((n_back >= 2) & (live() == 1))
            def _():
                score(2, 0, False)

                def pair(state):
                    m, _ = state
                    score(m + 1, 1, False)
                    apply(m, 0, False)

                    @pl.when((live() == 1) & (m + 2 <= n_back))
                    def _():
                        score(m + 2, 0, False)
                        apply(m + 1, 1, False)

                    return m + 2, live()

                _, alive = lax.while_loop(lambda st: (st[0] + 1 <= n_back) & (st[1] == 1), pair,
                                          (jnp.int32(2), jnp.int32(1)))

                @pl.when((alive == 1) & (n_back % 2 == 0))
                def _():
                    apply(n_back, 0, False)

                @pl.when((alive == 1) & (n_back % 2 == 1))
                def _():
                    apply(n_back, 1, False)

        for t in range(ATTN_HEADS // per_tile):
            o = acc_ref[t * per_tile]
            for h in range(1, per_tile):
                o = jnp.where((lane // HEAD_DIM) == h, acc_ref[t * per_tile + h], o)
            o_ref[0, pl.ds(q0, blk), t * LANES:(t + 1) * LANES] = o.astype(BF16)
        return 0

    lax.fori_loop(0, s_len // blk, q_block, 0)


def _attention(qkv):
    b, s, _ = qkv.shape
    blk = min(ATTN_BLOCK, s)
    width = ATTN_HEADS * HEAD_DIM
    n_steps = D_ATTN // width
    return pl.pallas_call(
        functools.partial(_attn_kernel, blk=blk),
        grid=(b, n_steps),
        in_specs=[
            pl.BlockSpec((1, s, width), lambda i, p: (i, 0, p)),
            pl.BlockSpec((1, s, width), lambda i, p: (i, 0, n_steps + p)),
            pl.BlockSpec((1, s, width), lambda i, p: (i, 0, 2 * n_steps + p)),
        ],
        out_specs=pl.BlockSpec((1, s, width), lambda i, p: (i, 0, p)),
        out_shape=jax.ShapeDtypeStruct((b, s, D_ATTN), BF16),
        scratch_shapes=[pltpu.VMEM((ATTN_HEADS, blk, LANES), F32),
                        pltpu.VMEM((ATTN_HEADS, blk, LANES), F32),
                        pltpu.VMEM((2, ATTN_HEADS, blk, blk), F32),
                        pltpu.VMEM((2, ATTN_HEADS, blk, blk), BF16)],
        compiler_params=pltpu.CompilerParams(
            dimension_semantics=("arbitrary", "arbitrary"), vmem_limit_bytes=VMEM_LIMIT),
    )(qkv, qkv, qkv)


def _oproj_kernel(o_ref, x_ref, w_ref, lng_ref, lnb_ref, rw_ref, rb_ref, tri_ref,
                  h_ref, hp_ref, route_ref, route_t_ref, counts_ref, cnt_ref):
    mix = jnp.dot(o_ref[0], w_ref[...], preferred_element_type=F32)
    _finish(x_ref[0], mix, _is_first_step(), lng_ref, lnb_ref, rw_ref, rb_ref, tri_ref,
            h_ref, hp_ref, route_ref, route_t_ref, counts_ref, cnt_ref)


def _attn_out(o, x, w_o, ln_g, ln_b, router):
    b, s, d = x.shape
    ts = min(ROW_TILE, s)
    f_in, f_out, f_shape, f_scratch = _finish_specs(b, s, d, ts)
    return pl.pallas_call(
        _oproj_kernel,
        grid=(b, s // ts),
        in_specs=[
            pl.BlockSpec((1, ts, D_ATTN), lambda i, j: (i, j, 0)),
            pl.BlockSpec((1, ts, d), lambda i, j: (i, j, 0)),
            _const_spec((D_ATTN, d)),
            *f_in,
        ],
        out_specs=f_out,
        out_shape=f_shape,
        scratch_shapes=f_scratch,
        compiler_params=pltpu.CompilerParams(
            dimension_semantics=("arbitrary", "arbitrary"), vmem_limit_bytes=VMEM_LIMIT),
    )(o, x, w_o.astype(BF16), *_finish_args(ln_g, ln_b, router, ts))


def _dispatch_meta(route_t, counts, chunk):
    t = route_t.shape[1]
    sizes = counts[0, N_GROUPS:N_GROUPS + N_EXPERTS].astype(I32)
    ids = route_t[ROUTE_ID:ROUTE_ID + TOP_K].astype(I32)
    rank = route_t[ROUTE_RANK:ROUTE_RANK + TOP_K].astype(I32)
    padded = (sizes + chunk - 1) // chunk * chunk
    pad_end = jnp.cumsum(padded)
    pad_start = pad_end - padded
    experts = jnp.arange(N_EXPERTS, dtype=I32)
    seg_start = jnp.sum(jnp.where(ids[None] == experts[:, None, None], pad_start[:, None, None], 0), axis=0)
    dest = (seg_start + rank).astype(I32)
    n_chunks = -(-(t * TOP_K) // chunk) + N_EXPERTS
    chunk_first_row = jnp.arange(n_chunks, dtype=I32) * chunk
    chunk_expert = jnp.minimum(
        jnp.sum((chunk_first_row[:, None] >= pad_end[None, :]).astype(I32), axis=1), N_EXPERTS - 1)
    chunk_rows = jnp.clip(
        pad_start[chunk_expert] + sizes[chunk_expert] - chunk_first_row, 0, chunk).astype(I32)
    n_valid = (pad_end[-1] // chunk).astype(I32).reshape(1)
    return dest, chunk_expert, chunk_rows, n_valid, n_chunks


def _index_windows(idx):
    return jnp.pad(idx.reshape(-1, SC_WINDOW), ((0, 0), (0, LANES - SC_WINDOW)))


def _sc_row_scatter(x, dest, n_out):
    t, d = x.shape
    mesh = plsc.VectorSubcoreMesh(core_axis_name="core", subcore_axis_name="subcore")
    idx = [_index_windows(dest[k]) for k in range(TOP_K)]

    @pl.kernel(out_type=jax.ShapeDtypeStruct((n_out, d), x.dtype), mesh=mesh, scratch_types=[])
    def scatter(x_hbm, *rest):
        i_hbm, o_hbm = rest[:TOP_K], rest[TOP_K]

        def body(x_vmem, *i_vmem):
            for k in range(TOP_K):
                pltpu.sync_copy(x_vmem, o_hbm.at[i_vmem[k].at[0, pl.ds(0, SC_WINDOW)]])

        pltpu.emit_pipeline(
            body,
            grid=(t // SC_WINDOW,),
            in_specs=[pl.BlockSpec((SC_WINDOW, d), lambda i: (i, 0))]
            + [pl.BlockSpec((1, LANES), lambda i: (i, 0))] * TOP_K,
            out_specs=[],
            core_axis_name=("core", "subcore"),
            dimension_semantics=(pltpu.PARALLEL,),
        )(x_hbm, *i_hbm)

    return scatter(x, *idx)


def _sc_row_gather(x, idx):
    _, d = x.shape
    m = idx.shape[0]
    mesh = plsc.VectorSubcoreMesh(core_axis_name="core", subcore_axis_name="subcore")

    @pl.kernel(out_type=jax.ShapeDtypeStruct((m, d), x.dtype), mesh=mesh, scratch_types=[])
    def gather(x_hbm, i_hbm, o_hbm):
        def body(i_vmem, o_vmem):
            pltpu.sync_copy(x_hbm.at[i_vmem.at[0, pl.ds(0, SC_WINDOW)]], o_vmem)

        pltpu.emit_pipeline(
            body,
            grid=(m // SC_WINDOW,),
            in_specs=[pl.BlockSpec((1, LANES), lambda i: (i, 0))],
            out_specs=[pl.BlockSpec((SC_WINDOW, d), lambda i: (i, 0))],
            core_axis_name=("core", "subcore"),
            dimension_semantics=(pltpu.PARALLEL,),
        )(i_hbm, o_hbm)

    return gather(x, _index_windows(idx))


def _expert_kernel(ce_ref, rows_ref, nv_ref, x_ref, wup_ref, wdn_ref, y_ref, wup_bf, wdn_bf):
    i = pl.program_id(0)

    @pl.when(i < nv_ref[0])
    def _():
        prev = ce_ref[jnp.maximum(i - 1, 0)]

        @pl.when((i == 0) | (ce_ref[i] != prev))
        def _():
            wup_bf[...] = wup_ref[0, 0].astype(BF16)
            wdn_bf[...] = wdn_ref[0, 0].astype(BF16)

        row = lax.broadcasted_iota(I32, x_ref.shape, 0)
        packed = jnp.where(row < rows_ref[i], x_ref[...], jnp.uint32(0))
        gu = jnp.dot(_unpack_bf16_pairs(packed).astype(BF16), wup_bf[...], preferred_element_type=F32)
        gate, up = gu[:, :D_EXPERT], gu[:, D_EXPERT:]
        act = gate * jax.nn.sigmoid(gate) * up
        y = jnp.dot(act.astype(BF16), wdn_bf[...], preferred_element_type=F32)
        y_ref[...] = _pack_bf16_pairs(y)

    @pl.when(i >= nv_ref[0])
    def _():
        y_ref[...] = jnp.zeros_like(y_ref)


def _experts(x_rows, chunk_expert, chunk_rows, n_valid, n_chunks, chunk, layer, w_up, w_down):
    _, half = x_rows.shape
    d = 2 * half
    grid_spec = pltpu.PrefetchScalarGridSpec(
        num_scalar_prefetch=3,
        grid=(n_chunks,),
        in_specs=[
            pl.BlockSpec((chunk, half), lambda i, ce, rows, nv: (jnp.minimum(i, nv[0] - 1), 0)),
            pl.BlockSpec((1, 1, d, 2 * D_EXPERT), lambda i, ce, rows, nv: (layer, ce[i], 0, 0)),
            pl.BlockSpec((1, 1, D_EXPERT, d), lambda i, ce, rows, nv: (layer, ce[i], 0, 0)),
        ],
        out_specs=pl.BlockSpec((chunk, half), lambda i, ce, rows, nv: (i, 0)),
        scratch_shapes=[
            pltpu.VMEM((d, 2 * D_EXPERT), BF16),
            pltpu.VMEM((D_EXPERT, d), BF16),
        ],
    )
    return pl.pallas_call(
        _expert_kernel,
        grid_spec=grid_spec,
        out_shape=jax.ShapeDtypeStruct((n_chunks * chunk, half), U32),
        compiler_params=pltpu.CompilerParams(
            dimension_semantics=("arbitrary",), vmem_limit_bytes=VMEM_LIMIT),
    )(chunk_expert, chunk_rows, n_valid, x_rows, w_up, w_down)


def _combine_kernel(h_ref, route_ref, *rest):
    y_refs, (lng_ref, lnb_ref), o_ref = rest[:TOP_K], rest[TOP_K:TOP_K + 2], rest[-1]
    gates = route_ref[:, ROUTE_GATE:ROUTE_GATE + TOP_K]
    ffn = gates[:, 0:1] * _unpack_bf16_pairs(y_refs[0][...])
    for k in range(1, TOP_K):
        ffn = ffn + gates[:, k:k + 1] * _unpack_bf16_pairs(y_refs[k][...])
    o_ref[...] = _layer_norm(DEEPNORM_ALPHA * h_ref[...] + ffn, lng_ref[...], lnb_ref[...])


def _combine_piece(h2d, route, y_piece, ln_g, ln_b, piece, out_so_far):
    t, d = h2d.shape
    rows = t // COMBINE_PIECES
    tm = min(COMBINE_TILE, rows)
    tiles = rows // tm
    row = lambda v: v.reshape(1, -1)
    here = lambda i: (piece * tiles + i, 0)
    y_specs = [pl.BlockSpec((tm, d // 2), functools.partial(lambda i, k: (k * tiles + i, 0), k=k))
               for k in range(TOP_K)]
    operands = [h2d, route, *([y_piece] * TOP_K), row(ln_g), row(ln_b)]
    in_specs = [pl.BlockSpec((tm, d), here), pl.BlockSpec((tm, ROUTER_COLS), here),
                *y_specs, _const_spec((1, d)), _const_spec((1, d))]
    aliases = {}
    if out_so_far is not None:
        aliases = {len(operands): 0}
        operands.append(out_so_far)
        in_specs.append(pl.BlockSpec(memory_space=pl.ANY))
    return pl.pallas_call(
        _combine_kernel,
        grid=(tiles,),
        in_specs=in_specs,
        out_specs=pl.BlockSpec((tm, d), here),
        out_shape=jax.ShapeDtypeStruct((t, d), F32),
        input_output_aliases=aliases,
        compiler_params=pltpu.CompilerParams(
            dimension_semantics=("arbitrary",), vmem_limit_bytes=VMEM_LIMIT),
    )(*operands)


def _moe_block(h, h_packed, route, route_t, counts, layer, w_up, w_down, ln_g, ln_b):
    b, s, d = h.shape
    t = b * s
    chunk = min(EXPERT_CHUNK, t)
    dest, chunk_expert, chunk_rows, n_valid, n_chunks = _dispatch_meta(route_t, counts, chunk)
    x_rows = _sc_row_scatter(h_packed.reshape(t, d // 2), dest, n_chunks * chunk)
    y_rows = _experts(x_rows, chunk_expert, chunk_rows, n_valid, n_chunks, chunk, layer, w_up, w_down)
    rows = t // COMBINE_PIECES
    out = None
    for piece in range(COMBINE_PIECES):
        y_piece = _sc_row_gather(y_rows, dest[:, piece * rows:(piece + 1) * rows].reshape(-1))
        out = _combine_piece(h.reshape(t, d), route.reshape(t, ROUTER_COLS), y_piece, ln_g, ln_b,
                             piece, out)
    return out.reshape(b, s, d)


def _router_params(rg_w, rg_b, re_w, re_b):
    w = jnp.concatenate([rg_w, re_w], axis=1)
    w = jnp.pad(w, ((0, 0), (0, ROUTER_COLS - w.shape[1])))
    hi = w.astype(BF16)
    lo = (w - hi.astype(F32)).astype(BF16)
    bias = jnp.concatenate([rg_b, re_b])
    bias = jnp.pad(bias, (0, ROUTER_COLS - bias.shape[0])).reshape(1, ROUTER_COLS)
    return jnp.concatenate([hi, lo], axis=1), bias


def kernel(x, even_w_in, even_conv_a, even_conv_b_w, even_conv_b_bias, even_norm_b_g,
           even_norm_b_b, even_w_out, odd_w_qkv, odd_w_o, ln_mix_g, ln_mix_b, ln_ffn_g,
           ln_ffn_b, router_group_w, router_group_b, router_expert_w, router_expert_b,
           expert_w_up, expert_w_down):
    for layer in range(DEPTH):
        i = layer // 2
        router = _router_params(router_group_w[layer], router_group_b[layer],
                                router_expert_w[layer], router_expert_b[layer])
        if layer % 2 == 0:
            mixed = _even_mixer(
                x, even_w_in[i], even_conv_a[i], even_conv_b_w[i], even_conv_b_bias[i],
                even_norm_b_g[i], even_norm_b_b[i], even_w_out[i], ln_mix_g[layer], ln_mix_b[layer],
                router)
        else:
            mixed = _attn_out(_attention(_qkv_proj(x, odd_w_qkv[i])), x, odd_w_o[i],
                              ln_mix_g[layer], ln_mix_b[layer], router)
        x = _moe_block(*mixed, layer, expert_w_up, expert_w_down, ln_ffn_g[layer], ln_ffn_b[layer])
    return x
```

```python
import functools

import jax
import jax.numpy as jnp
from jax import lax
from jax.experimental import pallas as pl
from jax.experimental.pallas import tpu as pltpu
from jax.experimental.pallas import tpu_sc as plsc

F32 = jnp.float32
BF16 = jnp.bfloat16
U32 = jnp.uint32
I32 = jnp.int32

D_MODEL = 1024
DEPTH = 4
D_CONV_A = 512
CONV_A_WIDTH = 3
D_CONV_B = 512
CONV_B_WIDTH = 31
N_HEADS = 16
HEAD_DIM = 64
D_ATTN = N_HEADS * HEAD_DIM
N_GROUPS = 4
EXPERTS_PER_GROUP = 8
N_EXPERTS = N_GROUPS * EXPERTS_PER_GROUP
TOP_K = 2
D_EXPERT = 512
LN_EPS = 1e-5
DEEPNORM_ALPHA = (2 * DEPTH) ** 0.25
D_IN_EVEN = 3 * D_CONV_A + 2 * D_CONV_B

LANES = 128
SUBLANES = 8
ROW_TILE = 512
QKV_TILE = 1024
COMBINE_TILE = 1024
CONV_HALO = 32
CONV_ROWS = 128
EVEN_SECTION = 128
ATTN_BLOCK = 256
ATTN_HEADS = 8
EXPERT_CHUNK = 1024
SC_WINDOW = 64
COMBINE_PIECES = 1
ROUTER_COLS = LANES
VMEM_LIMIT = 56 * 1024 * 1024
NEG_LARGE = -1e30
SOFTPLUS_LINEAR = 80.0
DECAY_STOP = 104.0

ROUTE_ID, ROUTE_GATE, ROUTE_RANK = 0, TOP_K, 2 * TOP_K
ROUTE_ROWS = 8


def _layer_norm(v, g, b):
    mu = jnp.mean(v, axis=-1, keepdims=True)
    c = v - mu
    var = jnp.mean(c * c, axis=-1, keepdims=True)
    return c * lax.rsqrt(var + LN_EPS) * g + b


def _pack_bf16_pairs(v):
    half = v.shape[1] // 2
    hi = lax.bitcast_convert_type(v[:, :half].astype(BF16).astype(F32), U32)
    lo = lax.bitcast_convert_type(v[:, half:].astype(BF16).astype(F32), U32)
    return hi | (lo >> 16)


def _unpack_bf16_pairs(w):
    hi = lax.bitcast_convert_type(w & jnp.uint32(0xFFFF0000), F32)
    lo = lax.bitcast_convert_type(w << 16, F32)
    return jnp.concatenate([hi, lo], axis=1)


def _router_logits(h, rw_ref):
    hi = h.astype(BF16)
    lo = (h - hi.astype(F32)).astype(BF16)
    r1 = jnp.dot(hi, rw_ref[...], preferred_element_type=F32)
    r2 = jnp.dot(lo, rw_ref[:, :ROUTER_COLS], preferred_element_type=F32)
    return r1[:, :ROUTER_COLS] + r1[:, ROUTER_COLS:] + r2


def _route(logits, tri_ref, base):
    lane = lax.broadcasted_iota(I32, logits.shape, 1)
    lane_f = lane.astype(F32)

    def top(vals):
        m = jnp.max(vals, axis=1, keepdims=True)
        return m, jnp.min(jnp.where(vals == m, lane_f, float(LANES)), axis=1, keepdims=True)

    is_group = lane < N_GROUPS
    g_max, grp = top(jnp.where(is_group, logits, NEG_LARGE))
    g_gate = 1.0 / jnp.sum(jnp.where(is_group, jnp.exp(logits - g_max), 0.0), axis=1, keepdims=True)
    first = N_GROUPS + EXPERTS_PER_GROUP * grp
    e_logits = jnp.where((lane_f >= first) & (lane_f < first + EXPERTS_PER_GROUP), logits, NEG_LARGE)
    v0, l0 = top(e_logits)
    v1, l1 = top(jnp.where(lane_f == l0, NEG_LARGE, e_logits))
    e = jnp.exp(v1 - v0)
    gate0 = g_gate / (1.0 + e)
    gate1 = gate0 * e

    hot0 = jnp.where(lane_f == l0, 1.0, 0.0)
    hot1 = jnp.where(lane_f == l1, 1.0, 0.0)
    before0 = jnp.dot(tri_ref[...], hot0.astype(BF16), preferred_element_type=F32)
    before1 = jnp.dot(tri_ref[...], hot1.astype(BF16), preferred_element_type=F32)
    n0 = jnp.sum(hot0, axis=0, keepdims=True)
    n1 = jnp.sum(hot1, axis=0, keepdims=True)
    rank0 = jnp.sum(hot0 * (before0 + base), axis=1, keepdims=True)
    rank1 = jnp.sum(hot1 * (before1 + base + n0), axis=1, keepdims=True)

    fields = [l0 - N_GROUPS, l1 - N_GROUPS, gate0, gate1, rank0, rank1]
    record = jnp.zeros(logits.shape, F32)
    for i, f in enumerate(fields):
        record = jnp.where(lane == i, f, record)
    return record, base + n0 + n1


def _finish(x, mix, is_first_step, lng_ref, lnb_ref, rw_ref, rb_ref, tri_ref,
            h_ref, hp_ref, route_ref, route_t_ref, counts_ref, cnt_ref):
    h = _layer_norm(DEEPNORM_ALPHA * x + mix, lng_ref[...], lnb_ref[...])
    h_ref[0] = h
    hp_ref[0] = _pack_bf16_pairs(h)

    @pl.when(is_first_step)
    def _():
        cnt_ref[...] = jnp.zeros_like(cnt_ref)

    record, counts = _route(_router_logits(h, rw_ref) + rb_ref[...], tri_ref, cnt_ref[...])
    route_ref[0] = record
    route_t_ref[...] = record.T[:ROUTE_ROWS]
    cnt_ref[...] = counts
    counts_ref[...] = counts


def _const_spec(shape):
    return pl.BlockSpec(shape, lambda *_: (0,) * len(shape))


def _finish_specs(b, s, d, ts):
    in_specs = [_const_spec((1, d)), _const_spec((1, d)), _const_spec((d, 2 * ROUTER_COLS)),
                _const_spec((1, ROUTER_COLS)), _const_spec((ts, ts))]
    tile = lambda w: pl.BlockSpec((1, ts, w), lambda i, j: (i, j, 0))
    out_specs = [tile(d), tile(d // 2), tile(ROUTER_COLS),
                 pl.BlockSpec((ROUTE_ROWS, ts), lambda i, j: (0, i * (s // ts) + j)),
                 _const_spec((1, ROUTER_COLS))]
    out_shape = [jax.ShapeDtypeStruct((b, s, d), F32), jax.ShapeDtypeStruct((b, s, d // 2), U32),
                 jax.ShapeDtypeStruct((b, s, ROUTER_COLS), F32),
                 jax.ShapeDtypeStruct((ROUTE_ROWS, b * s), F32),
                 jax.ShapeDtypeStruct((1, ROUTER_COLS), F32)]
    return in_specs, out_specs, out_shape, [pltpu.VMEM((1, ROUTER_COLS), F32)]


def _finish_args(ln_g, ln_b, router, ts):
    rw, rb = router
    tri = (lax.broadcasted_iota(I32, (ts, ts), 1) < lax.broadcasted_iota(I32, (ts, ts), 0)).astype(BF16)
    return ln_g.reshape(1, -1), ln_b.reshape(1, -1), rw, rb, tri


def _is_first_step():
    return (pl.program_id(0) == 0) & (pl.program_id(1) == 0)


class _ShiftedWindow:
    def __init__(self, buf, start, back):
        self.back = back
        self.window = buf[start - back:start + CONV_ROWS, :]
        self.rolled = {0: self.window}

    def rows(self, d):
        first = self.back + d
        shift = first % SUBLANES
        if shift not in self.rolled:
            n, c = self.window.shape
            tiles = pltpu.roll(self.window.reshape(n // SUBLANES, SUBLANES, c), SUBLANES - shift, axis=1)
            sub = lax.broadcasted_iota(I32, (1, SUBLANES, 1), 1)
            self.rolled[shift] = jnp.where(sub < SUBLANES - shift, tiles[:-1], tiles[1:]).reshape(n - SUBLANES, c)
        return self.rolled[shift][first - shift:first - shift + CONV_ROWS]


def _even_kernel(x_ref, win_ref, ca_ref, cbw_ref, cbb_ref, nbg_ref, nbb_ref, wout_ref,
                 lng_ref, lnb_ref, rw_ref, rb_ref, tri_ref,
                 h_ref, hp_ref, route_ref, route_t_ref, counts_ref,
                 cnt_ref, u_buf, cv_buf, g_buf, y_buf):
    s = pl.program_id(1)
    ts = x_ref.shape[1]
    x = x_ref[0]

    @pl.when(s == 0)
    def _():
        cv_buf[0:CONV_HALO] = jnp.zeros((CONV_HALO, D_CONV_A), F32)
        g_buf[0:CONV_HALO] = jnp.zeros((CONV_HALO, D_CONV_B), F32)

    @pl.when(s > 0)
    def _():
        cv_buf[0:CONV_HALO] = cv_buf[ts:ts + CONV_HALO]
        g_buf[0:CONV_HALO] = g_buf[ts:ts + CONV_HALO]

    c0 = D_CONV_A
    sec = min(EVEN_SECTION, ts)

    def project(first_row):
        rows = slice(first_row, first_row + sec)
        halo_rows = slice(CONV_HALO + first_row, CONV_HALO + first_row + sec)
        u_buf[rows] = jnp.dot(x_ref[0, rows, :].astype(BF16), win_ref[...], preferred_element_type=F32)
        cv_buf[halo_rows] = u_buf[rows, c0:2 * c0] * u_buf[rows, 2 * c0:3 * c0]
        b_val = u_buf[rows, 3 * c0:3 * c0 + D_CONV_B]
        b_gate = u_buf[rows, 3 * c0 + D_CONV_B:]
        g_buf[halo_rows] = b_val * jax.nn.sigmoid(b_gate)

    project(0)
    for r0 in range(0, ts, CONV_ROWS):
        if r0 % sec == 0 and r0 + sec < ts:
            project(r0 + sec)
        win_a = _ShiftedWindow(cv_buf, CONV_HALO + r0, SUBLANES)
        acc_a = None
        for k in range(CONV_A_WIDTH):
            term = ca_ref[k:k + 1, :] * win_a.rows(k - (CONV_A_WIDTH - 1))
            acc_a = term if acc_a is None else acc_a + term
        y_a = u_buf[r0:r0 + CONV_ROWS, 0:c0] * acc_a
        win_b = _ShiftedWindow(g_buf, CONV_HALO + r0, CONV_HALO)
        acc_b = None
        for k in range(CONV_B_WIDTH):
            term = cbw_ref[k:k + 1, :] * win_b.rows(k - (CONV_B_WIDTH - 1))
            acc_b = term if acc_b is None else acc_b + term
        gb = _layer_norm(acc_b + cbb_ref[...], nbg_ref[...], nbb_ref[...])
        y_b = gb * jax.nn.sigmoid(gb)
        y_buf[r0:r0 + CONV_ROWS, 0:c0] = y_a.astype(BF16)
        y_buf[r0:r0 + CONV_ROWS, c0:] = y_b.astype(BF16)

    mix = jnp.dot(y_buf[...], wout_ref[...], preferred_element_type=F32)
    _finish(x, mix, _is_first_step(), lng_ref, lnb_ref, rw_ref, rb_ref, tri_ref,
            h_ref, hp_ref, route_ref, route_t_ref, counts_ref, cnt_ref)


def _even_mixer(x, w_in, conv_a, conv_b_w, conv_b_bias, norm_b_g, norm_b_b, w_out, ln_g, ln_b, router):
    b, s, d = x.shape
    ts = min(ROW_TILE, s)
    row = lambda v: v.reshape(1, -1)
    f_in, f_out, f_shape, f_scratch = _finish_specs(b, s, d, ts)
    return pl.pallas_call(
        _even_kernel,
        grid=(b, s // ts),
        in_specs=[
            pl.BlockSpec((1, ts, d), lambda i, j: (i, j, 0)),
            _const_spec((d, D_IN_EVEN)),
            _const_spec((CONV_A_WIDTH, D_CONV_A)),
            _const_spec((CONV_B_WIDTH, D_CONV_B)),
            _const_spec((1, D_CONV_B)),
            _const_spec((1, D_CONV_B)),
            _const_spec((1, D_CONV_B)),
            _const_spec((D_CONV_A + D_CONV_B, d)),
            *f_in,
        ],
        out_specs=f_out,
        out_shape=f_shape,
        scratch_shapes=[
            *f_scratch,
            pltpu.VMEM((ts, D_IN_EVEN), F32),
            pltpu.VMEM((CONV_HALO + ts, D_CONV_A), F32),
            pltpu.VMEM((CONV_HALO + ts, D_CONV_B), F32),
            pltpu.VMEM((ts, D_CONV_A + D_CONV_B), BF16),
        ],
        compiler_params=pltpu.CompilerParams(
            dimension_semantics=("arbitrary", "arbitrary"), vmem_limit_bytes=VMEM_LIMIT),
    )(x, w_in.astype(BF16), conv_a, conv_b_w, row(conv_b_bias), row(norm_b_g), row(norm_b_b),
      w_out.astype(BF16), *_finish_args(ln_g, ln_b, router, ts))


def _qkv_kernel(x_ref, w_ref, o_ref):
    u = jnp.dot(x_ref[0].astype(BF16), w_ref[...], preferred_element_type=F32)
    o_ref[0, :, :D_ATTN] = (u[:, :D_ATTN] * (HEAD_DIM ** -0.5)).astype(BF16)
    o_ref[0, :, D_ATTN:] = u[:, D_ATTN:].astype(BF16)


def _qkv_proj(x, w_qkv):
    b, s, d = x.shape
    ts = min(QKV_TILE, s)
    return pl.pallas_call(
        _qkv_kernel,
        grid=(b, s // ts),
        in_specs=[pl.BlockSpec((1, ts, d), lambda i, j: (i, j, 0)), _const_spec((d, 3 * D_ATTN))],
        out_specs=pl.BlockSpec((1, ts, 3 * D_ATTN), lambda i, j: (i, j, 0)),
        out_shape=jax.ShapeDtypeStruct((b, s, 3 * D_ATTN), BF16),
        compiler_params=pltpu.CompilerParams(
            dimension_semantics=("arbitrary", "arbitrary"), vmem_limit_bytes=VMEM_LIMIT),
    )(x, w_qkv.astype(BF16))


def _attn_kernel(q_ref, k_ref, v_ref, o_ref, acc_ref, carry_ref, z_buf, sp_buf, *, blk):
    s_len = q_ref.shape[1]
    per_tile = LANES // HEAD_DIM
    lane = lax.broadcasted_iota(jnp.int32, (1, LANES), 1)
    tri = (lax.broadcasted_iota(jnp.int32, (blk, blk), 0)
           >= lax.broadcasted_iota(jnp.int32, (blk, blk), 1)).astype(BF16)
    causal = (lax.broadcasted_iota(jnp.int32, (blk, blk), 1)
              < lax.broadcasted_iota(jnp.int32, (blk, blk), 0))

    def tile_of(h):
        return slice((h // per_tile) * LANES, (h // per_tile + 1) * LANES)

    def q_block(n_back, _):
        q0 = pl.multiple_of(n_back * blk, blk)
        qms = [jnp.where((lane // HEAD_DIM) == h % per_tile, q_ref[0, pl.ds(q0, blk), tile_of(h)],
                         jnp.zeros((), BF16)) for h in range(ATTN_HEADS)]
        acc_ref[...] = jnp.zeros_like(acc_ref)
        carry_ref[...] = jnp.zeros_like(carry_ref)

        def score(m, slot, diagonal):
            k0 = pl.multiple_of(q0 - m * blk, blk)
            for h in range(ATTN_HEADS):
                kb = k_ref[0, pl.ds(k0, blk), tile_of(h)]
                z = lax.dot_general(qms[h], kb, (((1,), (1,)), ((), ())),
                                    preferred_element_type=F32)
                sp = jnp.where(z > SOFTPLUS_LINEAR, z, jnp.log(1.0 + jnp.exp(z)))
                if diagonal:
                    sp = jnp.where(causal, sp, 0.0)
                z_buf[slot, h] = z
                sp_buf[slot, h] = sp.astype(BF16)

        def apply(m, slot, diagonal):
            k0 = pl.multiple_of(q0 - m * blk, blk)
            for h in range(ATTN_HEADS):
                vb = v_ref[0, pl.ds(k0, blk), tile_of(h)]
                c = jnp.dot(sp_buf[slot, h], tri, preferred_element_type=F32)
                carry = carry_ref[h]
                a = jnp.exp(z_buf[slot, h] - c - jnp.concatenate([carry] * (blk // LANES), axis=1))
                if diagonal:
                    a = jnp.where(causal, a, 0.0)
                acc_ref[h] += jnp.dot(a.astype(BF16), vb, preferred_element_type=F32)
                carry_ref[h] = carry + jnp.broadcast_to(c[:, 0:1], carry.shape)

        @pl.when(n_back == 0)
        def _():
            score(0, 0, True)
            apply(0, 0, True)

        def live():
            return (jnp.min(carry_ref[...]) <= DECAY_STOP).astype(jnp.int32)

        @pl.when(n_back > 0)
        def _():
            score(0, 0, True)
            score(1, 1, False)
            apply(0, 0, True)
            apply(1, 1, False)

            @pl.when((n_back >= 2) & (live() == 1))
            def _():
                score(2, 0, False)

                def pair(state):
                    m, _ = state
                    score(m + 1, 1, False)
                    apply(m, 0, False)

                    @pl.when((live() == 1) & (m + 2 <= n_back))
                    def _():
                        score(m + 2, 0, False)
                        apply(m + 1, 1, False)

                    return m + 2, live()

                _, alive = lax.while_loop(lambda st: (st[0] + 1 <= n_back) & (st[1] == 1), pair,
                                          (jnp.int32(2), jnp.int32(1)))

                @pl.when((alive == 1) & (n_back % 2 == 0))
                def _():
                    apply(n_back, 0, False)

                @pl.when((alive == 1) & (n_back % 2 == 1))
                def _():
                    apply(n_back, 1, False)

        for t in range(ATTN_HEADS // per_tile):
            o = acc_ref[t * per_tile]
            for h in range(1, per_tile):
                o = jnp.where((lane // HEAD_DIM) == h, acc_ref[t * per_tile + h], o)
            o_ref[0, pl.ds(q0, blk), t * LANES:(t + 1) * LANES] = o.astype(BF16)
        return 0

    lax.fori_loop(0, s_len // blk, q_block, 0)


def _attention(qkv):
    b, s, _ = qkv.shape
    blk = min(ATTN_BLOCK, s)
    width = ATTN_HEADS * HEAD_DIM
    n_steps = D_ATTN // width
    return pl.pallas_call(
        functools.partial(_attn_kernel, blk=blk),
        grid=(b, n_steps),
        in_specs=[
            pl.BlockSpec((1, s, width), lambda i, p: (i, 0, p)),
            pl.BlockSpec((1, s, width), lambda i, p: (i, 0, n_steps + p)),
            pl.BlockSpec((1, s, width), lambda i, p: (i, 0, 2 * n_steps + p)),
        ],
        out_specs=pl.BlockSpec((1, s, width), lambda i, p: (i, 0, p)),
        out_shape=jax.ShapeDtypeStruct((b, s, D_ATTN), BF16),
        scratch_shapes=[pltpu.VMEM((ATTN_HEADS, blk, LANES), F32),
                        pltpu.VMEM((ATTN_HEADS, blk, LANES), F32),
                        pltpu.VMEM((2, ATTN_HEADS, blk, blk), F32),
                        pltpu.VMEM((2, ATTN_HEADS, blk, blk), BF16)],
        compiler_params=pltpu.CompilerParams(
            dimension_semantics=("arbitrary", "arbitrary"), vmem_limit_bytes=VMEM_LIMIT),
    )(qkv, qkv, qkv)


def _oproj_kernel(o_ref, x_ref, w_ref, lng_ref, lnb_ref, rw_ref, rb_ref, tri_ref,
                  h_ref, hp_ref, route_ref, route_t_ref, counts_ref, cnt_ref):
    mix = jnp.dot(o_ref[0], w_ref[...], preferred_element_type=F32)
    _finish(x_ref[0], mix, _is_first_step(), lng_ref, lnb_ref, rw_ref, rb_ref, tri_ref,
            h_ref, hp_ref, route_ref, route_t_ref, counts_ref, cnt_ref)


def _attn_out(o, x, w_o, ln_g, ln_b, router):
    b, s, d = x.shape
    ts = min(ROW_TILE, s)
    f_in, f_out, f_shape, f_scratch = _finish_specs(b, s, d, ts)
    return pl.pallas_call(
        _oproj_kernel,
        grid=(b, s // ts),
        in_specs=[
            pl.BlockSpec((1, ts, D_ATTN), lambda i, j: (i, j, 0)),
            pl.BlockSpec((1, ts, d), lambda i, j: (i, j, 0)),
            _const_spec((D_ATTN, d)),
            *f_in,
        ],
        out_specs=f_out,
        out_shape=f_shape,
        scratch_shapes=f_scratch,
        compiler_params=pltpu.CompilerParams(
            dimension_semantics=("arbitrary", "arbitrary"), vmem_limit_bytes=VMEM_LIMIT),
    )(o, x, w_o.astype(BF16), *_finish_args(ln_g, ln_b, router, ts))


def _dispatch_meta(route_t, counts, chunk):
    t = route_t.shape[1]
    sizes = counts[0, N_GROUPS:N_GROUPS + N_EXPERTS].astype(I32)
    ids = route_t[ROUTE_ID:ROUTE_ID + TOP_K].astype(I32)
    rank = route_t[ROUTE_RANK:ROUTE_RANK + TOP_K].astype(I32)
    padded = (sizes + chunk - 1) // chunk * chunk
    pad_end = jnp.cumsum(padded)
    pad_start = pad_end - padded
    experts = jnp.arange(N_EXPERTS, dtype=I32)
    seg_start = jnp.sum(jnp.where(ids[None] == experts[:, None, None], pad_start[:, None, None], 0), axis=0)
    dest = (seg_start + rank).astype(I32)
    n_chunks = -(-(t * TOP_K) // chunk) + N_EXPERTS
    chunk_first_row = jnp.arange(n_chunks, dtype=I32) * chunk
    chunk_expert = jnp.minimum(
        jnp.sum((chunk_first_row[:, None] >= pad_end[None, :]).astype(I32), axis=1), N_EXPERTS - 1)
    chunk_rows = jnp.clip(
        pad_start[chunk_expert] + sizes[chunk_expert] - chunk_first_row, 0, chunk).astype(I32)
    n_valid = (pad_end[-1] // chunk).astype(I32).reshape(1)
    return dest, chunk_expert, chunk_rows, n_valid, n_chunks


def _index_windows(idx):
    return jnp.pad(idx.reshape(-1, SC_WINDOW), ((0, 0), (0, LANES - SC_WINDOW)))


def _sc_row_scatter(x, dest, n_out):
    t, d = x.shape
    mesh = plsc.VectorSubcoreMesh(core_axis_name="core", subcore_axis_name="subcore")
    idx = [_index_windows(dest[k]) for k in range(TOP_K)]

    @pl.kernel(out_type=jax.ShapeDtypeStruct((n_out, d), x.dtype), mesh=mesh, scratch_types=[])
    def scatter(x_hbm, *rest):
        i_hbm, o_hbm = rest[:TOP_K], rest[TOP_K]

        def body(x_vmem, *i_vmem):
            for k in range(TOP_K):
                pltpu.sync_copy(x_vmem, o_hbm.at[i_vmem[k].at[0, pl.ds(0, SC_WINDOW)]])

        pltpu.emit_pipeline(
            body,
            grid=(t // SC_WINDOW,),
            in_specs=[pl.BlockSpec((SC_WINDOW, d), lambda i: (i, 0))]
            + [pl.BlockSpec((1, LANES), lambda i: (i, 0))] * TOP_K,
            out_specs=[],
            core_axis_name=("core", "subcore"),
            dimension_semantics=(pltpu.PARALLEL,),
        )(x_hbm, *i_hbm)

    return scatter(x, *idx)


def _sc_row_gather(x, idx):
    _, d = x.shape
    m = idx.shape[0]
    mesh = plsc.VectorSubcoreMesh(core_axis_name="core", subcore_axis_name="subcore")

    @pl.kernel(out_type=jax.ShapeDtypeStruct((m, d), x.dtype), mesh=mesh, scratch_types=[])
    def gather(x_hbm, i_hbm, o_hbm):
        def body(i_vmem, o_vmem):
            pltpu.sync_copy(x_hbm.at[i_vmem.at[0, pl.ds(0, SC_WINDOW)]], o_vmem)

        pltpu.emit_pipeline(
            body,
            grid=(m // SC_WINDOW,),
            in_specs=[pl.BlockSpec((1, LANES), lambda i: (i, 0))],
            out_specs=[pl.BlockSpec((SC_WINDOW, d), lambda i: (i, 0))],
            core_axis_name=("core", "subcore"),
            dimension_semantics=(pltpu.PARALLEL,),
        )(i_hbm, o_hbm)

    return gather(x, _index_windows(idx))


def _expert_kernel(ce_ref, rows_ref, nv_ref, x_ref, wup_ref, wdn_ref, y_ref, wup_bf, wdn_bf):
    i = pl.program_id(0)

    @pl.when(i < nv_ref[0])
    def _():
        prev = ce_ref[jnp.maximum(i - 1, 0)]

        @pl.when((i == 0) | (ce_ref[i] != prev))
        def _():
            wup_bf[...] = wup_ref[0, 0].astype(BF16)
            wdn_bf[...] = wdn_ref[0, 0].astype(BF16)

        row = lax.broadcasted_iota(I32, x_ref.shape, 0)
        packed = jnp.where(row < rows_ref[i], x_ref[...], jnp.uint32(0))
        gu = jnp.dot(_unpack_bf16_pairs(packed).astype(BF16), wup_bf[...], preferred_element_type=F32)
        gate, up = gu[:, :D_EXPERT], gu[:, D_EXPERT:]
        act = gate * jax.nn.sigmoid(gate) * up
        y = jnp.dot(act.astype(BF16), wdn_bf[...], preferred_element_type=F32)
        y_ref[...] = _pack_bf16_pairs(y)

    @pl.when(i >= nv_ref[0])
    def _():
        y_ref[...] = jnp.zeros_like(y_ref)


def _experts(x_rows, chunk_expert, chunk_rows, n_valid, n_chunks, chunk, layer, w_up, w_down):
    _, half = x_rows.shape
    d = 2 * half
    grid_spec = pltpu.PrefetchScalarGridSpec(
        num_scalar_prefetch=3,
        grid=(n_chunks,),
        in_specs=[
            pl.BlockSpec((chunk, half), lambda i, ce, rows, nv: (jnp.minimum(i, nv[0] - 1), 0)),
            pl.BlockSpec((1, 1, d, 2 * D_EXPERT), lambda i, ce, rows, nv: (layer, ce[i], 0, 0)),
            pl.BlockSpec((1, 1, D_EXPERT, d), lambda i, ce, rows, nv: (layer, ce[i], 0, 0)),
        ],
        out_specs=pl.BlockSpec((chunk, half), lambda i, ce, rows, nv: (i, 0)),
        scratch_shapes=[
            pltpu.VMEM((d, 2 * D_EXPERT), BF16),
            pltpu.VMEM((D_EXPERT, d), BF16),
        ],
    )
    return pl.pallas_call(
        _expert_kernel,
        grid_spec=grid_spec,
        out_shape=jax.ShapeDtypeStruct((n_chunks * chunk, half), U32),
        compiler_params=pltpu.CompilerParams(
            dimension_semantics=("arbitrary",), vmem_limit_bytes=VMEM_LIMIT),
    )(chunk_expert, chunk_rows, n_valid, x_rows, w_up, w_down)


def _combine_kernel(h_ref, route_ref, *rest):
    y_refs, (lng_ref, lnb_ref), o_ref = rest[:TOP_K], rest[TOP_K:TOP_K + 2], rest[-1]
    gates = route_ref[:, ROUTE_GATE:ROUTE_GATE + TOP_K]
    ffn = gates[:, 0:1] * _unpack_bf16_pairs(y_refs[0][...])
    for k in range(1, TOP_K):
        ffn = ffn + gates[:, k:k + 1] * _unpack_bf16_pairs(y_refs[k][...])
    o_ref[...] = _layer_norm(DEEPNORM_ALPHA * h_ref[...] + ffn, lng_ref[...], lnb_ref[...])


def _combine_piece(h2d, route, y_piece, ln_g, ln_b, piece, out_so_far):
    t, d = h2d.shape
    rows = t // COMBINE_PIECES
    tm = min(COMBINE_TILE, rows)
    tiles = rows // tm
    row = lambda v: v.reshape(1, -1)
    here = lambda i: (piece * tiles + i, 0)
    y_specs = [pl.BlockSpec((tm, d // 2), functools.partial(lambda i, k: (k * tiles + i, 0), k=k))
               for k in range(TOP_K)]
    operands = [h2d, route, *([y_piece] * TOP_K), row(ln_g), row(ln_b)]
    in_specs = [pl.BlockSpec((tm, d), here), pl.BlockSpec((tm, ROUTER_COLS), here),
                *y_specs, _const_spec((1, d)), _const_spec((1, d))]
    aliases = {}
    if out_so_far is not None:
        aliases = {len(operands): 0}
        operands.append(out_so_far)
        in_specs.append(pl.BlockSpec(memory_space=pl.ANY))
    return pl.pallas_call(
        _combine_kernel,
        grid=(tiles,),
        in_specs=in_specs,
        out_specs=pl.BlockSpec((tm, d), here),
        out_shape=jax.ShapeDtypeStruct((t, d), F32),
        input_output_aliases=aliases,
        compiler_params=pltpu.CompilerParams(
            dimension_semantics=("arbitrary",), vmem_limit_bytes=VMEM_LIMIT),
    )(*operands)


def _moe_block(h, h_packed, route, route_t, counts, layer, w_up, w_down, ln_g, ln_b):
    b, s, d = h.shape
    t = b * s
    chunk = min(EXPERT_CHUNK, t)
    dest, chunk_expert, chunk_rows, n_valid, n_chunks = _dispatch_meta(route_t, counts, chunk)
    x_rows = _sc_row_scatter(h_packed.reshape(t, d // 2), dest, n_chunks * chunk)
    y_rows = _experts(x_rows, chunk_expert, chunk_rows, n_valid, n_chunks, chunk, layer, w_up, w_down)
    rows = t // COMBINE_PIECES
    out = None
    for piece in range(COMBINE_PIECES):
        y_piece = _sc_row_gather(y_rows, dest[:, piece * rows:(piece + 1) * rows].reshape(-1))
        out = _combine_piece(h.reshape(t, d), route.reshape(t, ROUTER_COLS), y_piece, ln_g, ln_b,
                             piece, out)
    return out.reshape(b, s, d)


def _router_params(rg_w, rg_b, re_w, re_b):
    w = jnp.concatenate([rg_w, re_w], axis=1)
    w = jnp.pad(w, ((0, 0), (0, ROUTER_COLS - w.shape[1])))
    hi = w.astype(BF16)
    lo = (w - hi.astype(F32)).astype(BF16)
    bias = jnp.concatenate([rg_b, re_b])
    bias = jnp.pad(bias, (0, ROUTER_COLS - bias.shape[0])).reshape(1, ROUTER_COLS)
    return jnp.concatenate([hi, lo], axis=1), bias


def kernel(x, even_w_in, even_conv_a, even_conv_b_w, even_conv_b_bias, even_norm_b_g,
           even_norm_b_b, even_w_out, odd_w_qkv, odd_w_o, ln_mix_g, ln_mix_b, ln_ffn_g,
           ln_ffn_b, router_group_w, router_group_b, router_expert_w, router_expert_b,
           expert_w_up, expert_w_down):
    for layer in range(DEPTH):
        i = layer // 2
        router = _router_params(router_group_w[layer], router_group_b[layer],
                                router_expert_w[layer], router_expert_b[layer])
        if layer % 2 == 0:
            mixed = _even_mixer(
                x, even_w_in[i], even_conv_a[i], even_conv_b_w[i], even_conv_b_bias[i],
                even_norm_b_g[i], even_norm_b_b[i], even_w_out[i], ln_mix_g[layer], ln_mix_b[layer],
                router)
        else:
            mixed = _attn_out(_attention(_qkv_proj(x, odd_w_qkv[i])), x, odd_w_o[i],
                              ln_mix_g[layer], ln_mix_b[layer], router)
        x = _moe_block(*mixed, layer, expert_w_up, expert_w_down, ln_ffn_g[layer], ln_ffn_b[layer])
    return x
```
